```python
import jax, jax.numpy as jnp
from jax import lax
import numpy as np

D_MODEL = 2048
BATCH = 1
SEQ = 8192
DEPTH = 2

M_HEADS = 4
M_DV = D_MODEL // M_HEADS
M_DQK = M_DV // 2
M_CHUNK = 64
CONV_K = 4
SB_HEADS = 16
SB_DH = D_MODEL // SB_HEADS
SB_BLOCK = 128
N_EXPERTS = 32
N_GROUPS = 4
EXPERTS_PER_GROUP = N_EXPERTS // N_GROUPS
TOP_K = 2
D_FF_EXPERT = 768
MOE_BLOCK = 128
EPS = 1e-6
NEG = -1e30

M_QK_W = M_HEADS * M_DQK
M_V_W = M_HEADS * M_DV
SB_W = SB_HEADS * SB_DH
SPLITS = (M_QK_W, M_QK_W, M_V_W, M_V_W, M_HEADS, M_HEADS, SB_W, SB_W, SB_W, M_V_W, SB_W)
D_IN = sum(SPLITS)

kernel_name = "hybrid_mlstm_stickbreaking_groupmoe_adaln"


def rms_norm(x, g):
    xf = x.astype(jnp.float32)
    y = xf * lax.rsqrt(jnp.mean(xf * xf, axis=-1, keepdims=True) + EPS)
    return y.astype(x.dtype) * g


def causal_depthwise_conv(x, w, b):
    y = lax.conv_general_dilated(x, w[:, None, :], window_strides=(1,),
                                 padding=[(CONV_K - 1, 0)],
                                 dimension_numbers=('NWC', 'WIO', 'NWC'),
                                 feature_group_count=x.shape[-1])
    return y + b


def mlstm_chunkwise(q, k, v, i_pre, f_pre):
    B, H, S, _ = q.shape
    nc = S // M_CHUNK

    def chunks(a):
        a = a.reshape(a.shape[:2] + (nc, M_CHUNK) + a.shape[3:])
        return jnp.moveaxis(a, 2, 0)

    log_f = jax.nn.log_sigmoid(f_pre)
    causal = jnp.tril(jnp.ones((M_CHUNK, M_CHUNK), dtype=bool))

    def step(carry, inp):
        C, n, m = carry
        qb, kb, vb, ib, fb = inp
        g = jnp.cumsum(fb, axis=-1)
        dmat = g[..., :, None] - g[..., None, :] + ib[..., None, :]
        dmat = jnp.where(causal, dmat, NEG)
        inter = g + m[..., None]
        m_row = jnp.maximum(inter, jnp.max(dmat, axis=-1))
        w_intra = jnp.exp(dmat - m_row[..., None])
        a_inter = jnp.exp(inter - m_row)
        s_qk = jnp.einsum('bhld,bhsd->bhls', qb, kb) * w_intra
        num = (jnp.einsum('bhls,bhsv->bhlv', s_qk, vb)
               + a_inter[..., None] * jnp.einsum('bhld,bhdv->bhlv', qb, C))
        den = jnp.sum(s_qk, axis=-1) + a_inter * jnp.einsum('bhld,bhd->bhl', qb, n)
        h = num / jnp.maximum(jnp.abs(den), jnp.exp(-m_row))[..., None]
        g_last = g[..., -1]
        w_k = g_last[..., None] - g + ib
        m_new = jnp.maximum(g_last + m, jnp.max(w_k, axis=-1))
        decay = jnp.exp(g_last + m - m_new)
        w_k = jnp.exp(w_k - m_new[..., None])
        C_new = decay[..., None, None] * C + jnp.einsum('bhs,bhsd,bhsv->bhdv', w_k, kb, vb)
        n_new = decay[..., None] * n + jnp.einsum('bhs,bhsd->bhd', w_k, kb)
        return (C_new, n_new, m_new), h

    init = (jnp.zeros((B, H, M_DQK, M_DV), jnp.float32),
            jnp.zeros((B, H, M_DQK), jnp.float32),
            jnp.zeros((B, H), jnp.float32))
    _, h = lax.scan(step, init, (chunks(q), chunks(k), chunks(v), chunks(i_pre), chunks(log_f)))
    return jnp.moveaxis(h, 0, 2).reshape(B, H, S, M_DV)


def stick_breaking(q, k, v):
    S = q.shape[2]
    nb = S // SB_BLOCK
    q = q * (SB_DH ** -0.5)
    key_pos = jnp.arange(S)

    def block(bi):
        qb = lax.dynamic_slice_in_dim(q, bi * SB_BLOCK, SB_BLOCK, axis=2)
        z = jnp.einsum('bhqd,bhkd->bhqk', qb, k).astype(jnp.float32)
        q_pos = bi * SB_BLOCK + jnp.arange(SB_BLOCK)
        mask = key_pos[None, :] < q_pos[:, None]
        log_keep = jnp.where(mask, jax.nn.log_sigmoid(-z), 0.0)
        suffix = lax.cumsum(log_keep, axis=3, reverse=True) - log_keep
        a = jnp.where(mask, jnp.exp(jax.nn.log_sigmoid(z) + suffix), 0.0)
        return jnp.einsum('bhqk,bhkd->bhqd', a, v)

    out = lax.map(block, jnp.arange(nb))
    return jnp.moveaxis(out, 0, 2).reshape(q.shape)


def hybrid_mixer(h, w_in, m_conv_w, m_conv_b, m_igate_b, m_fgate_b, m_norm_g, w_out):
    B, S, _ = h.shape
    p = h @ w_in
    idx = [int(i) for i in np.cumsum(SPLITS)[:-1]]
    mq, mk, mv, mo, mi, mf, sq, sk, sv, gm, gs = jnp.split(p, idx, axis=-1)

    def heads(a, nh, dh):
        return a.reshape(B, S, nh, dh).transpose(0, 2, 1, 3).astype(jnp.float32)

    qk = jax.nn.silu(causal_depthwise_conv(jnp.concatenate([mq, mk], axis=-1), m_conv_w, m_conv_b))
    mq, mk = jnp.split(qk, 2, axis=-1)
    q_m = heads(mq, M_HEADS, M_DQK) * (M_DQK ** -0.5)
    k_m = heads(mk, M_HEADS, M_DQK)
    v_m = heads(mv, M_HEADS, M_DV)
    i_pre = (mi + m_igate_b).astype(jnp.float32).transpose(0, 2, 1)
    f_pre = (mf + m_fgate_b).astype(jnp.float32).transpose(0, 2, 1)
    hm = mlstm_chunkwise(q_m, k_m, v_m, i_pre, f_pre)
    hm = hm * lax.rsqrt(jnp.mean(hm * hm, axis=-1, keepdims=True) + EPS)
    hm = hm.transpose(0, 2, 1, 3).reshape(B, S, M_V_W) * m_norm_g
    hm = jax.nn.sigmoid(mo.astype(jnp.float32)) * hm

    hs = stick_breaking(heads(sq, SB_HEADS, SB_DH), heads(sk, SB_HEADS, SB_DH), heads(sv, SB_HEADS, SB_DH))
    hs = hs.transpose(0, 2, 1, 3).reshape(B, S, SB_W)

    y = jax.nn.sigmoid(gm.astype(jnp.float32)) * hm + jax.nn.sigmoid(gs.astype(jnp.float32)) * hs
    return y.astype(h.dtype) @ w_out


def grouped_moe(h, router_w, router_bias, w_gate, w_up, w_down):
    B, S, D = h.shape
    T = B * S
    x2 = h.reshape(T, D)
    scores = jax.nn.sigmoid((x2 @ router_w).astype(jnp.float32))
    sel = (scores + router_bias.astype(jnp.float32)).reshape(T, N_GROUPS, EXPERTS_PER_GROUP)
    group_score = jnp.sum(lax.top_k(sel, TOP_K)[0], axis=-1)
    group = jnp.argmax(group_score, axis=-1)
    sel_in = jnp.take_along_axis(sel, group[:, None, None], axis=1)[:, 0]
    local = lax.top_k(sel_in, TOP_K)[1]
    expert = group[:, None] * EXPERTS_PER_GROUP + local
    w = jnp.take_along_axis(scores, expert, axis=1)
    w = w / jnp.sum(w, axis=-1, keepdims=True)

    n_assign = T * TOP_K
    flat_e = expert.reshape(-1)
    flat_t = jnp.repeat(jnp.arange(T, dtype=jnp.int32), TOP_K)
    flat_w = w.reshape(-1)
    order = jnp.argsort(flat_e)
    e_sorted = flat_e[order]
    counts = jnp.bincount(flat_e, length=N_EXPERTS)
    starts = jnp.cumsum(counts) - counts
    padded = (counts + MOE_BLOCK - 1) // MOE_BLOCK * MOE_BLOCK
    padded_ends = jnp.cumsum(padded)
    padded_starts = padded_ends - padded
    dest = padded_starts[e_sorted] + jnp.arange(n_assign) - starts[e_sorted]
    n_pad = -(-n_assign // MOE_BLOCK) * MOE_BLOCK + N_EXPERTS * MOE_BLOCK
    n_blocks = n_pad // MOE_BLOCK
    slot_tok = jnp.zeros((n_pad,), jnp.int32).at[dest].set(flat_t[order])
    slot_w = jnp.zeros((n_pad,), jnp.float32).at[dest].set(flat_w[order])
    block_e = jnp.minimum(jnp.searchsorted(padded_ends, jnp.arange(n_blocks) * MOE_BLOCK, side='right'),
                          N_EXPERTS - 1)
    xs = x2[slot_tok].reshape(n_blocks, MOE_BLOCK, D)

    def expert_block(args):
        xb, e = args
        return (jax.nn.silu(xb @ w_gate[e]) * (xb @ w_up[e])) @ w_down[e]

    ys = lax.map(expert_block, (xs, block_e)).reshape(n_pad, D)
    out = jnp.zeros((T, D), jnp.float32).at[slot_tok].add(ys.astype(jnp.float32) * slot_w[:, None])
    return out.astype(h.dtype).reshape(B, S, D)


def setup_inputs(seed: int = 0) -> dict:
    key = jax.random.key(seed)
    ks = jax.random.split(key, 20)

    def nrm(k, shape, s):
        return jax.random.normal(k, shape, jnp.float32) * s

    x = nrm(ks[0], (BATCH, SEQ, D_MODEL), 1.0)
    c = nrm(ks[1], (BATCH, D_MODEL), 1.0)
    ada_w = nrm(ks[2], (DEPTH, D_MODEL, 6 * D_MODEL), 0.5 * D_MODEL ** -0.5)
    ada_b = nrm(ks[3], (DEPTH, 6 * D_MODEL), 0.02)
    norm1_g = 1.0 + nrm(ks[4], (DEPTH, D_MODEL), 0.02)
    w_in = nrm(ks[5], (DEPTH, D_MODEL, D_IN), D_MODEL ** -0.5)
    m_conv_w = nrm(ks[6], (DEPTH, CONV_K, 2 * M_QK_W), CONV_K ** -0.5)
    m_conv_b = nrm(ks[7], (DEPTH, 2 * M_QK_W), 0.02)
    m_igate_b = nrm(ks[8], (DEPTH, M_HEADS), 0.1)
    m_fgate_b = jnp.linspace(3.0, 6.0, M_HEADS, dtype=jnp.float32)[None, :] + nrm(ks[9], (DEPTH, M_HEADS), 0.1)
    m_norm_g = 1.0 + nrm(ks[10], (DEPTH, M_V_W), 0.02)
    w_out = nrm(ks[11], (DEPTH, D_MODEL, D_MODEL), D_MODEL ** -0.5)
    norm2_g = 1.0 + nrm(ks[12], (DEPTH, D_MODEL), 0.02)
    router_w = nrm(ks[13], (D_MODEL, N_EXPERTS), D_MODEL ** -0.5)
    router_bias = nrm(ks[14], (N_EXPERTS,), 0.01)
    w_gate = nrm(ks[15], (DEPTH, N_EXPERTS, D_MODEL, D_FF_EXPERT), D_MODEL ** -0.5)
    w_up = nrm(ks[16], (DEPTH, N_EXPERTS, D_MODEL, D_FF_EXPERT), D_MODEL ** -0.5)
    w_down = nrm(ks[17], (DEPTH, N_EXPERTS, D_FF_EXPERT, D_MODEL), D_FF_EXPERT ** -0.5)
    final_g = 1.0 + nrm(ks[18], (D_MODEL,), 0.02)
    return {'x': x, 'c': c, 'ada_w': ada_w, 'ada_b': ada_b, 'norm1_g': norm1_g, 'w_in': w_in,
            'm_conv_w': m_conv_w, 'm_conv_b': m_conv_b, 'm_igate_b': m_igate_b, 'm_fgate_b': m_fgate_b,
            'm_norm_g': m_norm_g, 'w_out': w_out, 'norm2_g': norm2_g, 'router_w': router_w,
            'router_bias': router_bias, 'w_gate': w_gate, 'w_up': w_up, 'w_down': w_down,
            'final_g': final_g}


def reference(x, c, ada_w, ada_b, norm1_g, w_in, m_conv_w, m_conv_b, m_igate_b, m_fgate_b,
              m_norm_g, w_out, norm2_g, router_w, router_bias, w_gate, w_up, w_down, final_g):
    c_act = jax.nn.silu(c)
    for l in range(DEPTH):
        mod = c_act @ ada_w[l] + ada_b[l]
        sh1, sc1, g1, sh2, sc2, g2 = jnp.split(mod[:, None, :], 6, axis=-1)
        h = rms_norm(x, norm1_g[l]) * (1.0 + sc1) + sh1
        x = x + g1 * hybrid_mixer(h, w_in[l], m_conv_w[l], m_conv_b[l], m_igate_b[l], m_fgate_b[l],
                                  m_norm_g[l], w_out[l])
        h = rms_norm(x, norm2_g[l]) * (1.0 + sc2) + sh2
        x = x + g2 * grouped_moe(h, router_w, router_bias, w_gate[l], w_up[l], w_down[l])
    return rms_norm(x, final_g)
```

```python
import functools

import jax
import jax.numpy as jnp
from jax import lax
from jax.experimental import pallas as pl
from jax.experimental.pallas import tpu as pltpu

F32 = jnp.float32
BF16 = jnp.bfloat16

D_MODEL = 2048
DEPTH = 2
M_HEADS = 4
M_DV = D_MODEL // M_HEADS
M_DQK = M_DV // 2
CONV_K = 4
SB_HEADS = 16
SB_DH = D_MODEL // SB_HEADS
N_EXPERTS = 32
N_GROUPS = 4
EXPERTS_PER_GROUP = N_EXPERTS // N_GROUPS
TOP_K = 2
D_FF_EXPERT = 768
EPS = 1e-6
NEG = -1e30

M_QK_W = M_HEADS * M_DQK
M_V_W = M_HEADS * M_DV
SB_W = SB_HEADS * SB_DH
GATE_COL0 = 2 * M_QK_W + 2 * M_V_W
GATE_COLS = 2 * M_HEADS
P_MQ, P_MK, P_MV, P_MO = 0, M_QK_W, 2 * M_QK_W, 2 * M_QK_W + M_V_W
P_SQ = GATE_COL0
P_SK, P_SV, P_GM, P_GS = P_SQ + SB_W, P_SQ + 2 * SB_W, P_SQ + 3 * SB_W, P_SQ + 4 * SB_W
P_W = P_SQ + 5 * SB_W

LANES = 128
SUBLANES = 8
VMEM_LIMIT = 56 * 1024 * 1024

MLSTM_CHUNK = 256
SB_TILE = 256
MOE_BM = 512
MOE_FF_CHUNK = 256
COMBINE_TC = 256


def _cparams(sem):
    return pltpu.CompilerParams(dimension_semantics=sem, vmem_limit_bytes=VMEM_LIMIT)


def _ada_body(c_ref, w_ref, b_ref, o_ref):
    c = c_ref[...]
    cs = c * jax.nn.sigmoid(c)
    for j in range(w_ref.shape[2] // LANES):
        sl = slice(j * LANES, (j + 1) * LANES)
        o_ref[0, :, sl] = jnp.sum(w_ref[0, :, sl] * cs, axis=0, keepdims=True) + b_ref[0, :, sl]


def _ada(c, ada_w, ada_b):
    depth, d, n = ada_w.shape
    tn = 1024
    cb = jnp.broadcast_to(c.reshape(d, 1), (d, LANES))
    return pl.pallas_call(
        _ada_body,
        out_shape=jax.ShapeDtypeStruct((depth, 1, n), F32),
        grid=(depth, n // tn),
        in_specs=[pl.BlockSpec((d, LANES), lambda l, j: (0, 0)),
                  pl.BlockSpec((1, d, tn), lambda l, j: (l, 0, j)),
                  pl.BlockSpec((1, 1, tn), lambda l, j: (l, 0, j))],
        out_specs=pl.BlockSpec((1, 1, tn), lambda l, j: (l, 0, j)),
        compiler_params=_cparams(("arbitrary", "arbitrary")),
        name="ada_mod",
    )(cb, ada_w, ada_b.reshape(depth, 1, n))


def _norm_mod(x, g, sc, sh):
    y = x * lax.rsqrt(jnp.mean(x * x, axis=-1, keepdims=True) + EPS)
    return (y * g) * (1.0 + sc) + sh


def _final_norm_body(x_ref, g_ref, o_ref):
    x = x_ref[...]
    o_ref[...] = (x * lax.rsqrt(jnp.mean(x * x, axis=-1, keepdims=True) + EPS)) * g_ref[...]


def _final_norm(x, g):
    s, d = x.shape
    tm = min(512, s)
    return pl.pallas_call(
        _final_norm_body,
        out_shape=jax.ShapeDtypeStruct((s, d), F32),
        grid=(s // tm,),
        in_specs=[pl.BlockSpec((tm, d), lambda i: (i, 0)), pl.BlockSpec((1, d), lambda i: (0, 0))],
        out_specs=pl.BlockSpec((tm, d), lambda i: (i, 0)),
        compiler_params=_cparams(("arbitrary",)),
        name="final_norm",
    )(x, g.reshape(1, d))


def _in_proj_body(x_ref, g_ref, sc_ref, sh_ref, w_ref, wif_ref, p_ref, gate_ref, h_ref):
    @pl.when(pl.program_id(1) == 0)
    def _():
        h = _norm_mod(x_ref[...], g_ref[...], sc_ref[...], sh_ref[...]).astype(BF16)
        h_ref[...] = h
        gate_ref[...] = jnp.dot(h, wif_ref[...], preferred_element_type=F32)

    p_ref[...] = jnp.dot(h_ref[...], w_ref[...], preferred_element_type=F32).astype(p_ref.dtype)


def _in_proj(x, g, sc, sh, w_cat, w_if):
    s, d = x.shape
    n = w_cat.shape[1]
    tm, tn = min(1024, s), 1024
    vec = pl.BlockSpec((1, d), lambda i, j: (0, 0))
    return pl.pallas_call(
        _in_proj_body,
        out_shape=(jax.ShapeDtypeStruct((s, n), BF16), jax.ShapeDtypeStruct((s, LANES), F32)),
        grid=(s // tm, n // tn),
        in_specs=[pl.BlockSpec((tm, d), lambda i, j: (i, 0)), vec, vec, vec,
                  pl.BlockSpec((d, tn), lambda i, j: (0, j)),
                  pl.BlockSpec((d, LANES), lambda i, j: (0, 0))],
        out_specs=(pl.BlockSpec((tm, tn), lambda i, j: (i, j)),
                   pl.BlockSpec((tm, LANES), lambda i, j: (i, 0))),
        scratch_shapes=[pltpu.VMEM((tm, d), BF16)],
        compiler_params=_cparams(("arbitrary", "arbitrary")),
        name="in_proj",
    )(x, g, sc, sh, w_cat, w_if)


def _gates_body(g_ref, b_ref, o_ref):
    v = g_ref[...] + b_ref[...]
    n = v.shape[1]
    logf = jnp.minimum(v, 0.0) - jnp.log(1.0 + jnp.exp(-jnp.abs(v)))
    upper = (lax.broadcasted_iota(jnp.int32, (n, n), 0)
             <= lax.broadcasted_iota(jnp.int32, (n, n), 1)).astype(F32)
    csum = jnp.dot(logf, upper, precision=lax.Precision.HIGHEST, preferred_element_type=F32)
    row = lax.broadcasted_iota(jnp.int32, v.shape, 0)
    o_ref[...] = jnp.where(row < M_HEADS, v, csum)


def _gates(gates_t, bias, chunk):
    r, s = gates_t.shape
    return pl.pallas_call(
        _gates_body,
        out_shape=jax.ShapeDtypeStruct((r, s), F32),
        grid=(s // chunk,),
        in_specs=[pl.BlockSpec((r, chunk), lambda c: (0, c)), pl.BlockSpec((r, 1), lambda c: (0, 0))],
        out_specs=pl.BlockSpec((r, chunk), lambda c: (0, c)),
        compiler_params=_cparams(("arbitrary",)),
        name="mlstm_gates",
    )(gates_t, bias)


def _conv_silu(x_ref, tail_ref, w_ref, b_ref):
    x = x_ref[...].astype(F32)
    n = x.shape[0]
    tail = tail_ref[...]
    w = w_ref[...]
    row8 = lax.broadcasted_iota(jnp.int32, tail.shape, 0)
    y = b_ref[...] + w[CONV_K - 1:CONV_K, :] * x
    for d in range(1, CONV_K):
        rolled = pltpu.roll(x, d, 0)
        head = jnp.where(row8 < d, pltpu.roll(tail, d, 0), rolled[:SUBLANES])
        xd = jnp.concatenate([head, rolled[SUBLANES:]], axis=0)
        y = y + w[CONV_K - 1 - d:CONV_K - d, :] * xd
    tail_ref[...] = x[n - SUBLANES:, :]
    return y * jax.nn.sigmoid(y)


def _mlstm_body(q_ref, k_ref, v_ref, cwq_ref, cwk_ref, cbq_ref, cbk_ref, gr_ref, gc_ref, o_ref,
                c_ref, m_ref, tq_ref, tk_ref):
    @pl.when(pl.program_id(1) == 0)
    def _():
        c_ref[...] = jnp.zeros_like(c_ref)
        m_ref[...] = jnp.zeros_like(m_ref)
        tq_ref[...] = jnp.zeros_like(tq_ref)
        tk_ref[...] = jnp.zeros_like(tk_ref)

    q = _conv_silu(q_ref, tq_ref, cwq_ref, cbq_ref) * (M_DQK ** -0.5)
    k = _conv_silu(k_ref, tk_ref, cwk_ref, cbk_ref)
    n = q.shape[0]
    qb = q.astype(BF16)
    kb = k.astype(BF16)
    vext = jnp.concatenate([v_ref[...], jnp.ones((n, LANES), BF16)], axis=1)

    gr = gr_ref[0]
    gc = gc_ref[0]
    i_row, g_row = gr[0:1, :], gr[1:2, :]
    i_col, g_col = gc[:, 0:1], gc[:, 1:2]
    m = m_ref[...]

    causal = (lax.broadcasted_iota(jnp.int32, (n, n), 1) <= lax.broadcasted_iota(jnp.int32, (n, n), 0))
    dmat = jnp.where(causal, g_col - g_row + i_row, NEG)
    inter = g_col + m
    m_row = jnp.maximum(inter, jnp.max(dmat, axis=-1, keepdims=True))
    w_intra = jnp.exp(dmat - m_row)
    a_inter = jnp.exp(inter - m_row)
    s_qk = lax.dot_general(qb, kb, (((1,), (1,)), ((), ())), preferred_element_type=F32) * w_intra
    state = c_ref[...]
    tot = (jnp.dot(s_qk.astype(BF16), vext, preferred_element_type=F32)
           + a_inter * jnp.dot(qb, state.astype(BF16), preferred_element_type=F32))
    num = tot[:, :M_DV]
    den = tot[:, M_DV:M_DV + 1]
    h = num / jnp.maximum(jnp.abs(den), jnp.exp(-m_row))
    h = h * lax.rsqrt(jnp.mean(h * h, axis=-1, keepdims=True) + EPS)
    o_ref[...] = h.astype(o_ref.dtype)

    g_last = g_col[n - 1:n, :]
    w_k = g_last - g_col + i_col
    m_new = jnp.maximum(g_last + m, jnp.max(w_k, axis=0, keepdims=True))
    decay = jnp.exp(g_last + m - m_new)
    kw = (k * jnp.exp(w_k - m_new)).astype(BF16)
    upd = lax.dot_general(kw, vext, (((0,), (0,)), ((), ())), preferred_element_type=F32)
    c_ref[...] = decay * state + upd
    m_ref[...] = m_new


def _mlstm(p, conv_w, conv_b, g_rows, g_cols, chunk):
    s = p.shape[0]
    nc = s // chunk
    kq, kv = P_MK // M_DQK, P_MV // M_DV
    return pl.pallas_call(
        _mlstm_body,
        out_shape=jax.ShapeDtypeStruct((s, M_V_W), BF16),
        grid=(M_HEADS, nc),
        in_specs=[pl.BlockSpec((chunk, M_DQK), lambda h, c: (c, h)),
                  pl.BlockSpec((chunk, M_DQK), lambda h, c: (c, kq + h)),
                  pl.BlockSpec((chunk, M_DV), lambda h, c: (c, kv + h)),
                  pl.BlockSpec((CONV_K, M_DQK), lambda h, c: (0, h)),
                  pl.BlockSpec((CONV_K, M_DQK), lambda h, c: (0, kq + h)),
                  pl.BlockSpec((1, M_DQK), lambda h, c: (0, h)),
                  pl.BlockSpec((1, M_DQK), lambda h, c: (0, kq + h)),
                  pl.BlockSpec((1, 2, chunk), lambda h, c: (h, 0, c)),
                  pl.BlockSpec((1, chunk, 2), lambda h, c: (h, c, 0))],
        out_specs=pl.BlockSpec((chunk, M_DV), lambda h, c: (c, h)),
        scratch_shapes=[pltpu.VMEM((M_DQK, M_DV + LANES), F32), pltpu.VMEM((1, 1), F32),
                        pltpu.VMEM((SUBLANES, M_DQK), F32), pltpu.VMEM((SUBLANES, M_DQK), F32)],
        compiler_params=_cparams(("arbitrary", "arbitrary")),
        name="mlstm",
    )(p, p, p, conv_w, conv_w, conv_b, conv_b, g_rows, g_cols)


def _sb_body(q_ref, k_ref, v_ref, o_ref, *, tile):
    i = pl.program_id(1)
    q = (q_ref[...].astype(F32) * (SB_DH ** -0.5)).astype(BF16)
    rows = lax.broadcasted_iota(jnp.int32, (tile, tile), 0)
    cols = lax.broadcasted_iota(jnp.int32, (tile, tile), 1)
    later = (rows > cols).astype(BF16)
    diag_mask = cols < rows

    def key_tile(j, rsum, acc, masked):
        kb = k_ref[pl.ds(pl.multiple_of(j * tile, tile), tile), :]
        vb = v_ref[pl.ds(pl.multiple_of(j * tile, tile), tile), :]
        z = lax.dot_general(q, kb, (((1,), (1,)), ((), ())), preferred_element_type=F32)
        sp = jnp.maximum(z, 0.0) + jnp.log(1.0 + jnp.exp(-jnp.abs(z)))
        if masked:
            sp = jnp.where(diag_mask, sp, 0.0)
        within = jnp.dot(sp.astype(BF16), later, preferred_element_type=F32)
        a = jnp.exp((z - sp) - within - rsum)
        if masked:
            a = jnp.where(diag_mask, a, 0.0)
        acc = acc + jnp.dot(a.astype(BF16), vb, preferred_element_type=F32)
        rsum = rsum + jnp.sum(sp, axis=-1, keepdims=True)
        return rsum, acc

    rsum, acc = key_tile(i, jnp.zeros((tile, 1), F32), jnp.zeros((tile, SB_DH), F32), True)

    def body(t, carry):
        return key_tile(i - 1 - t, carry[0], carry[1], False)

    rsum, acc = lax.fori_loop(0, i, body, (rsum, acc))
    o_ref[...] = acc.astype(o_ref.dtype)


def _stick_breaking(p, tile):
    s = p.shape[0]
    cq, ck, cv = P_SQ // SB_DH, P_SK // SB_DH, P_SV // SB_DH
    return pl.pallas_call(
        functools.partial(_sb_body, tile=tile),
        out_shape=jax.ShapeDtypeStruct((s, SB_W), BF16),
        grid=(SB_HEADS, s // tile),
        in_specs=[pl.BlockSpec((tile, SB_DH), lambda h, i: (i, cq + h)),
                  pl.BlockSpec((s, SB_DH), lambda h, i: (0, ck + h)),
                  pl.BlockSpec((s, SB_DH), lambda h, i: (0, cv + h))],
        out_specs=pl.BlockSpec((tile, SB_DH), lambda h, i: (i, h)),
        compiler_params=_cparams(("arbitrary", "arbitrary")),
        name="stick_breaking",
    )(p, p, p)


def _out_proj_body(mo_ref, gm_ref, gs_ref, hm_ref, hs_ref, ng_ref, x_ref, g1_ref, w_ref, o_ref):
    sig = jax.nn.sigmoid
    hm = sig(mo_ref[...].astype(F32)) * (hm_ref[...].astype(F32) * ng_ref[...])
    y = sig(gm_ref[...].astype(F32)) * hm + sig(gs_ref[...].astype(F32)) * hs_ref[...].astype(F32)
    o_ref[...] = x_ref[...] + g1_ref[...] * jnp.dot(y.astype(BF16), w_ref[...], preferred_element_type=F32)


def _out_proj(p, hm, hs, norm_g, x, g1, w_out):
    s, d = x.shape
    tm = min(256, s)
    row = lambda blk: pl.BlockSpec((tm, d), lambda i: (i, blk))
    vec = pl.BlockSpec((1, d), lambda i: (0, 0))
    return pl.pallas_call(
        _out_proj_body,
        out_shape=jax.ShapeDtypeStruct((s, d), F32),
        grid=(s // tm,),
        in_specs=[row(P_MO // d), row(P_GM // d), row(P_GS // d), row(0), row(0), vec, row(0), vec,
                  pl.BlockSpec((d, d), lambda i: (0, 0))],
        out_specs=row(0),
        compiler_params=_cparams(("arbitrary",)),
        name="out_proj",
    )(p, p, p, hm, hs, norm_g, x, g1, w_out)


def _router_body(x_ref, g_ref, sc_ref, sh_ref, rwt_ref, rb_ref, h_ref, eid_ref, wt_ref):
    h = _norm_mod(x_ref[...], g_ref[...], sc_ref[...], sh_ref[...])
    h_ref[...] = h
    logits = lax.dot_general(rwt_ref[...], h, (((1,), (1,)), ((), ())),
                             precision=lax.Precision.HIGHEST, preferred_element_type=F32)
    scores = jax.nn.sigmoid(logits)
    sel = scores + rb_ref[...]
    t = sel.shape[1]
    idx = lax.broadcasted_iota(jnp.int32, (EXPERTS_PER_GROUP, t), 0)
    best = None
    for g in range(N_GROUPS):
        sl = slice(g * EXPERTS_PER_GROUP, (g + 1) * EXPERTS_PER_GROUP)
        s, sc = sel[sl], scores[sl]
        m1 = jnp.max(s, axis=0, keepdims=True)
        i1 = jnp.min(jnp.where(s == m1, idx, EXPERTS_PER_GROUP), axis=0, keepdims=True)
        s2 = jnp.where(idx == i1, -jnp.inf, s)
        m2 = jnp.max(s2, axis=0, keepdims=True)
        i2 = jnp.min(jnp.where(s2 == m2, idx, EXPERTS_PER_GROUP), axis=0, keepdims=True)
        w1 = jnp.sum(jnp.where(idx == i1, sc, 0.0), axis=0, keepdims=True)
        w2 = jnp.sum(jnp.where(idx == i2, sc, 0.0), axis=0, keepdims=True)
        cand = (m1 + m2, i1 + g * EXPERTS_PER_GROUP, i2 + g * EXPERTS_PER_GROUP, w1, w2)
        if best is None:
            best = cand
        else:
            better = cand[0] > best[0]
            best = tuple(jnp.where(better, c, b) for c, b in zip(cand, best))
    _, e1, e2, w1, w2 = best
    wsum = w1 + w2
    eid_ref[...] = jnp.concatenate([e1, e2], axis=0)
    wt_ref[...] = jnp.concatenate([w1 / wsum, w2 / wsum], axis=0)


def _router(x, g, sc, sh, router_wt, router_bias):
    s, d = x.shape
    tm = min(512, s)
    vec = pl.BlockSpec((1, d), lambda i: (0, 0))
    return pl.pallas_call(
        _router_body,
        out_shape=(jax.ShapeDtypeStruct((s, d), F32), jax.ShapeDtypeStruct((TOP_K, s), jnp.int32),
                   jax.ShapeDtypeStruct((TOP_K, s), F32)),
        grid=(s // tm,),
        in_specs=[pl.BlockSpec((tm, d), lambda i: (i, 0)), vec, vec, vec,
                  pl.BlockSpec((N_EXPERTS, d), lambda i: (0, 0)),
                  pl.BlockSpec((N_EXPERTS, 1), lambda i: (0, 0))],
        out_specs=(pl.BlockSpec((tm, d), lambda i: (i, 0)), pl.BlockSpec((TOP_K, tm), lambda i: (0, i)),
                   pl.BlockSpec((TOP_K, tm), lambda i: (0, i))),
        compiler_params=_cparams(("arbitrary",)),
        name="router",
    )(x, g, sc, sh, router_wt, router_bias.reshape(N_EXPERTS, 1))


def _dispatch(eid, bm):
    s = eid.shape[1]
    n_assign = TOP_K * s
    flat_e = eid.reshape(-1)
    onehot = (flat_e[:, None] == jnp.arange(N_EXPERTS, dtype=jnp.int32)[None, :]).astype(jnp.int32)
    csum = jnp.cumsum(onehot, axis=0)
    rank = jnp.sum(csum * onehot, axis=1) - 1
    counts = csum[-1]
    padded = (counts + bm - 1) // bm * bm
    pends = jnp.cumsum(padded)
    dest = (pends - padded)[flat_e] + rank
    nb_max = pl.cdiv(n_assign, bm) + N_EXPERTS
    flat_t = jnp.tile(jnp.arange(s, dtype=jnp.int32), TOP_K)
    slot_tok = jnp.zeros((nb_max * bm,), jnp.int32).at[dest].set(flat_t)
    nb_used = (pends[-1] // bm).astype(jnp.int32).reshape(1)
    block_e = jnp.minimum(jnp.searchsorted(pends, jnp.arange(nb_max, dtype=jnp.int32) * bm, side='right'),
                          N_EXPERTS - 1).astype(jnp.int32)
    return dest.astype(jnp.int32), slot_tok, block_e, nb_used, nb_max


def _expert_body(be_ref, st_ref, nu_ref, h_hbm, wg_ref, wu_ref, wd_ref, y_ref, xg_ref, xb_ref, sem, *, bm):
    b = pl.program_id(0)
    f = pl.program_id(1)
    nb = nu_ref[0]

    def row_copy(blk, r, slot):
        tok = st_ref[blk * bm + r]
        return pltpu.make_async_copy(h_hbm.at[pl.ds(tok, 1), :], xg_ref.at[slot, pl.ds(r, 1), :], sem.at[slot])

    def start_gather(blk, slot):
        def go(r, c):
            row_copy(blk, r, slot).start()
            return c
        lax.fori_loop(0, bm, go, 0)

    def wait_gather(blk, slot):
        def go(r, c):
            row_copy(blk, r, slot).wait()
            return c
        lax.fori_loop(0, bm, go, 0)

    @pl.when(b < nb)
    def _():
        slot = b % 2

        @pl.when(f == 0)
        def _():
            @pl.when(b == 0)
            def _():
                start_gather(0, 0)

            @pl.when(b + 1 < nb)
            def _():
                start_gather(b + 1, 1 - slot)

            wait_gather(b, slot)
            xb_ref[...] = xg_ref[slot].astype(BF16)

        x = xb_ref[...]
        gate = jnp.dot(x, wg_ref[0].astype(BF16), preferred_element_type=F32)
        up = jnp.dot(x, wu_ref[0].astype(BF16), preferred_element_type=F32)
        mid = (gate * jax.nn.sigmoid(gate) * up).astype(BF16)
        y = jnp.dot(mid, wd_ref[0].astype(BF16), preferred_element_type=F32)

        @pl.when(f == 0)
        def _():
            y_ref[...] = y

        @pl.when(f != 0)
        def _():
            y_ref[...] += y

    @pl.when((b >= nb) & (f == 0))
    def _():
        y_ref[...] = jnp.zeros_like(y_ref)


def _experts(h, slot_tok, block_e, nb_used, nb_max, w_gate, w_up, w_down, bm):
    d = h.shape[1]
    fc = MOE_FF_CHUNK
    nf = D_FF_EXPERT // fc

    def blk(b, nu):
        return jnp.minimum(b, nu[0] - 1)

    def ffc(b, f, nu):
        return jnp.where(b < nu[0], f, nf - 1)

    grid_spec = pltpu.PrefetchScalarGridSpec(
        num_scalar_prefetch=3,
        grid=(nb_max, nf),
        in_specs=[pl.BlockSpec(memory_space=pl.ANY),
                  pl.BlockSpec((1, d, fc), lambda b, f, be, st, nu: (be[blk(b, nu)], 0, ffc(b, f, nu))),
                  pl.BlockSpec((1, d, fc), lambda b, f, be, st, nu: (be[blk(b, nu)], 0, ffc(b, f, nu))),
                  pl.BlockSpec((1, fc, d), lambda b, f, be, st, nu: (be[blk(b, nu)], ffc(b, f, nu), 0))],
        out_specs=pl.BlockSpec((bm, d), lambda b, f, be, st, nu: (b, 0)),
        scratch_shapes=[pltpu.VMEM((2, bm, d), F32), pltpu.VMEM((bm, d), BF16), pltpu.SemaphoreType.DMA((2,))],
    )
    return pl.pallas_call(
        functools.partial(_expert_body, bm=bm),
        out_shape=jax.ShapeDtypeStruct((nb_max * bm, d), F32),
        grid_spec=grid_spec,
        compiler_params=_cparams(("arbitrary", "arbitrary")),
        name="experts",
    )(block_e, slot_tok, nb_used, h, w_gate, w_up, w_down)


def _combine_body(pos_ref, y_hbm, w_ref, x_ref, g2_ref, o_ref, yb_ref, sem, *, tc, s):
    i = pl.program_id(0)
    n = pl.num_programs(0)

    def row_copy(blk, r, k, slot):
        src = pos_ref[k * s + blk * tc + r]
        return pltpu.make_async_copy(y_hbm.at[pl.ds(src, 1), :], yb_ref.at[slot, k, pl.ds(r, 1), :], sem.at[slot])

    def start_gather(blk, slot):
        def go(r, c):
            for k in range(TOP_K):
                row_copy(blk, r, k, slot).start()
            return c
        lax.fori_loop(0, tc, go, 0)

    def wait_gather(blk, slot):
        def go(r, c):
            for k in range(TOP_K):
                row_copy(blk, r, k, slot).wait()
            return c
        lax.fori_loop(0, tc, go, 0)

    slot = i % 2

    @pl.when(i == 0)
    def _():
        start_gather(0, 0)

    @pl.when(i + 1 < n)
    def _():
        start_gather(i + 1, 1 - slot)

    wait_gather(i, slot)
    w = w_ref[...]
    moe = w[:, 0:1] * yb_ref[slot, 0] + w[:, 1:2] * yb_ref[slot, 1]
    o_ref[...] = x_ref[...] + g2_ref[...] * moe


def _combine(ys, pos, wts, x, g2, tc):
    s, d = x.shape
    grid_spec = pltpu.PrefetchScalarGridSpec(
        num_scalar_prefetch=1,
        grid=(s // tc,),
        in_specs=[pl.BlockSpec(memory_space=pl.ANY),
                  pl.BlockSpec((tc, TOP_K), lambda i, pos: (i, 0)),
                  pl.BlockSpec((tc, d), lambda i, pos: (i, 0)),
                  pl.BlockSpec((1, d), lambda i, pos: (0, 0))],
        out_specs=pl.BlockSpec((tc, d), lambda i, pos: (i, 0)),
        scratch_shapes=[pltpu.VMEM((2, TOP_K, tc, d), F32), pltpu.SemaphoreType.DMA((2,))],
    )
    return pl.pallas_call(
        functools.partial(_combine_body, tc=tc, s=s),
        out_shape=jax.ShapeDtypeStruct((s, d), F32),
        grid_spec=grid_spec,
        compiler_params=_cparams(("arbitrary",)),
        name="moe_combine",
    )(pos, ys, wts, x, g2)


def kernel(x, c, ada_w, ada_b, norm1_g, w_in, m_conv_w, m_conv_b, m_igate_b, m_fgate_b, m_norm_g, w_out,
           norm2_g, router_w, router_bias, w_gate, w_up, w_down, final_g):
    batch, s, d = x.shape
    assert batch == 1 and d == D_MODEL
    chunk = min(MLSTM_CHUNK, s)
    sb_tile = min(SB_TILE, s)
    tc = min(COMBINE_TC, s)
    xs = x.reshape(s, d)
    mod = _ada(c, ada_w, ada_b)
    router_wt = router_w.T
    for l in range(DEPTH):
        sh1, sc1, g1, sh2, sc2, g2 = [mod[l, :, i * d:(i + 1) * d] for i in range(6)]
        w_cat = jnp.concatenate([w_in[l, :, :GATE_COL0], w_in[l, :, GATE_COL0 + GATE_COLS:]],
                                axis=1).astype(BF16)
        w_if = jnp.pad(w_in[l, :, GATE_COL0:GATE_COL0 + GATE_COLS],
                       ((0, 0), (0, LANES - GATE_COLS))).astype(BF16)
        p, gates = _in_proj(xs, norm1_g[l].reshape(1, d), sc1, sh1, w_cat, w_if)
        gate_bias = jnp.concatenate([m_igate_b[l], m_fgate_b[l]]).reshape(GATE_COLS, 1)
        gproc = _gates(gates[:, :GATE_COLS].T, gate_bias, chunk)
        g_rows = gproc.reshape(2, M_HEADS, s).transpose(1, 0, 2)
        g_cols = g_rows.transpose(0, 2, 1)
        hm = _mlstm(p, m_conv_w[l], m_conv_b[l].reshape(1, -1), g_rows, g_cols, chunk)
        hs = _stick_breaking(p, sb_tile)
        xs = _out_proj(p, hm, hs, m_norm_g[l].reshape(1, d), xs, g1, w_out[l].astype(BF16))
        h2, eid, wts = _router(xs, norm2_g[l].reshape(1, d), sc2, sh2, router_wt, router_bias)
        dest, slot_tok, block_e, nb_used, nb_max = _dispatch(eid, MOE_BM)
        ys = _experts(h2, slot_tok, block_e, nb_used, nb_max, w_gate[l], w_up[l], w_down[l], MOE_BM)
        xs = _combine(ys, dest, wts.T, xs, g2, tc)
    return _final_norm(xs, final_g).reshape(batch, s, d)
```

```python
import functools

import jax
import jax.numpy as jnp
from jax import lax
from jax.experimental import pallas as pl
from jax.experimental.pallas import tpu as pltpu

F32 = jnp.float32
BF16 = jnp.bfloat16

D_MODEL = 2048
DEPTH = 2
M_HEADS = 4
M_DV = D_MODEL // M_HEADS
M_DQK = M_DV // 2
CONV_K = 4
SB_HEADS = 16
SB_DH = D_MODEL // SB_HEADS
N_EXPERTS = 32
N_GROUPS = 4
EXPERTS_PER_GROUP = N_EXPERTS // N_GROUPS
TOP_K = 2
D_FF_EXPERT = 768
EPS = 1e-6
NEG = -1e30

M_QK_W = M_HEADS * M_DQK
M_V_W = M_HEADS * M_DV
SB_W = SB_HEADS * SB_DH
GATE_COL0 = 2 * M_QK_W + 2 * M_V_W
GATE_COLS = 2 * M_HEADS
P_MQ, P_MK, P_MV, P_MO = 0, M_QK_W, 2 * M_QK_W, 2 * M_QK_W + M_V_W
P_SQ = GATE_COL0
P_SK, P_SV, P_GM, P_GS = P_SQ + SB_W, P_SQ + 2 * SB_W, P_SQ + 3 * SB_W, P_SQ + 4 * SB_W
P_W = P_SQ + 5 * SB_W

LANES = 128
SUBLANES = 8
VMEM_LIMIT = 56 * 1024 * 1024

MLSTM_CHUNK = 256
SB_TILE = 256
MOE_BM = 512
MOE_FF_CHUNK = 256
COMBINE_TC = 256


def _cparams(sem):
    return pltpu.CompilerParams(dimension_semantics=sem, vmem_limit_bytes=VMEM_LIMIT)


def _ada_body(c_ref, w_ref, b_ref, o_ref):
    c = c_ref[...]
    cs = c * jax.nn.sigmoid(c)
    for j in range(w_ref.shape[2] // LANES):
        sl = slice(j * LANES, (j + 1) * LANES)
        o_ref[0, :, sl] = jnp.sum(w_ref[0, :, sl] * cs, axis=0, keepdims=True) + b_ref[0, :, sl]


def _ada(c, ada_w, ada_b):
    depth, d, n = ada_w.shape
    tn = 1024
    cb = jnp.broadcast_to(c.reshape(d, 1), (d, LANES))
    return pl.pallas_call(
        _ada_body,
        out_shape=jax.ShapeDtypeStruct((depth, 1, n), F32),
        grid=(depth, n // tn),
        in_specs=[pl.BlockSpec((d, LANES), lambda l, j: (0, 0)),
                  pl.BlockSpec((1, d, tn), lambda l, j: (l, 0, j)),
                  pl.BlockSpec((1, 1, tn), lambda l, j: (l, 0, j))],
        out_specs=pl.BlockSpec((1, 1, tn), lambda l, j: (l, 0, j)),
        compiler_params=_cparams(("arbitrary", "arbitrary")),
        name="ada_mod",
    )(cb, ada_w, ada_b.reshape(depth, 1, n))


def _norm_mod(x, g, sc, sh):
    y = x * lax.rsqrt(jnp.mean(x * x, axis=-1, keepdims=True) + EPS)
    return (y * g) * (1.0 + sc) + sh


def _final_norm_body(x_ref, g_ref, o_ref):
    x = x_ref[...]
    o_ref[...] = (x * lax.rsqrt(jnp.mean(x * x, axis=-1, keepdims=True) + EPS)) * g_ref[...]


def _final_norm(x, g):
    s, d = x.shape
    tm = min(512, s)
    return pl.pallas_call(
        _final_norm_body,
        out_shape=jax.ShapeDtypeStruct((s, d), F32),
        grid=(s // tm,),
        in_specs=[pl.BlockSpec((tm, d), lambda i: (i, 0)), pl.BlockSpec((1, d), lambda i: (0, 0))],
        out_specs=pl.BlockSpec((tm, d), lambda i: (i, 0)),
        compiler_params=_cparams(("arbitrary",)),
        name="final_norm",
    )(x, g.reshape(1, d))


def _in_proj_body(x_ref, g_ref, sc_ref, sh_ref, w_ref, wif_ref, p_ref, gate_ref, h_ref):
    @pl.when(pl.program_id(1) == 0)
    def _():
        h = _norm_mod(x_ref[...], g_ref[...], sc_ref[...], sh_ref[...]).astype(BF16)
        h_ref[...] = h
        gate_ref[...] = jnp.dot(h, wif_ref[...], preferred_element_type=F32)

    p_ref[...] = jnp.dot(h_ref[...], w_ref[...], preferred_element_type=F32).astype(p_ref.dtype)


def _in_proj(x, g, sc, sh, w_cat, w_if):
    s, d = x.shape
    n = w_cat.shape[1]
    tm, tn = min(1024, s), 1024
    vec = pl.BlockSpec((1, d), lambda i, j: (0, 0))
    return pl.pallas_call(
        _in_proj_body,
        out_shape=(jax.ShapeDtypeStruct((s, n), BF16), jax.ShapeDtypeStruct((s, LANES), F32)),
        grid=(s // tm, n // tn),
        in_specs=[pl.BlockSpec((tm, d), lambda i, j: (i, 0)), vec, vec, vec,
                  pl.BlockSpec((d, tn), lambda i, j: (0, j)),
                  pl.BlockSpec((d, LANES), lambda i, j: (0, 0))],
        out_specs=(pl.BlockSpec((tm, tn), lambda i, j: (i, j)),
                   pl.BlockSpec((tm, LANES), lambda i, j: (i, 0))),
        scratch_shapes=[pltpu.VMEM((tm, d), BF16)],
        compiler_params=_cparams(("arbitrary", "arbitrary")),
        name="in_proj",
    )(x, g, sc, sh, w_cat, w_if)


def _gates_body(g_ref, b_ref, o_ref):
    v = g_ref[...] + b_ref[...]
    n = v.shape[1]
    logf = jnp.minimum(v, 0.0) - jnp.log(1.0 + jnp.exp(-jnp.abs(v)))
    upper = (lax.broadcasted_iota(jnp.int32, (n, n), 0)
             <= lax.broadcasted_iota(jnp.int32, (n, n), 1)).astype(F32)
    csum = jnp.dot(logf, upper, precision=lax.Precision.HIGHEST, preferred_element_type=F32)
    row = lax.broadcasted_iota(jnp.int32, v.shape, 0)
    o_ref[...] = jnp.where(row < M_HEADS, v, csum)


def _gates(gates_t, bias, chunk):
    r, s = gates_t.shape
    return pl.pallas_call(
        _gates_body,
        out_shape=jax.ShapeDtypeStruct((r, s), F32),
        grid=(s // chunk,),
        in_specs=[pl.BlockSpec((r, chunk), lambda c: (0, c)), pl.BlockSpec((r, 1), lambda c: (0, 0))],
        out_specs=pl.BlockSpec((r, chunk), lambda c: (0, c)),
        compiler_params=_cparams(("arbitrary",)),
        name="mlstm_gates",
    )(gates_t, bias)


def _conv_silu(x_ref, tail_ref, w_ref, b_ref):
    x = x_ref[...].astype(F32)
    n = x.shape[0]
    tail = tail_ref[...]
    w = w_ref[...]
    row8 = lax.broadcasted_iota(jnp.int32, tail.shape, 0)
    y = b_ref[...] + w[CONV_K - 1:CONV_K, :] * x
    for d in range(1, CONV_K):
        rolled = pltpu.roll(x, d, 0)
        head = jnp.where(row8 < d, pltpu.roll(tail, d, 0), rolled[:SUBLANES])
        xd = jnp.concatenate([head, rolled[SUBLANES:]], axis=0)
        y = y + w[CONV_K - 1 - d:CONV_K - d, :] * xd
    tail_ref[...] = x[n - SUBLANES:, :]
    return y * jax.nn.sigmoid(y)


def _mlstm_body(q_ref, k_ref, v_ref, cwq_ref, cwk_ref, cbq_ref, cbk_ref, gr_ref, gc_ref, o_ref,
                c_ref, m_ref, tq_ref, tk_ref):
    @pl.when(pl.program_id(1) == 0)
    def _():
        c_ref[...] = jnp.zeros_like(c_ref)
        m_ref[...] = jnp.zeros_like(m_ref)
        tq_ref[...] = jnp.zeros_like(tq_ref)
        tk_ref[...] = jnp.zeros_like(tk_ref)

    q = _conv_silu(q_ref, tq_ref, cwq_ref, cbq_ref) * (M_DQK ** -0.5)
    k = _conv_silu(k_ref, tk_ref, cwk_ref, cbk_ref)
    n = q.shape[0]
    qb = q.astype(BF16)
    kb = k.astype(BF16)
    vext = jnp.concatenate([v_ref[...], jnp.ones((n, LANES), BF16)], axis=1)

    gr = gr_ref[0]
    gc = gc_ref[0]
    i_row, g_row = gr[0:1, :], gr[1:2, :]
    i_col, g_col = gc[:, 0:1], gc[:, 1:2]
    m = m_ref[...]

    causal = (lax.broadcasted_iota(jnp.int32, (n, n), 1) <= lax.broadcasted_iota(jnp.int32, (n, n), 0))
    dmat = jnp.where(causal, g_col - g_row + i_row, NEG)
    inter = g_col + m
    m_row = jnp.maximum(inter, jnp.max(dmat, axis=-1, keepdims=True))
    w_intra = jnp.exp(dmat - m_row)
    a_inter = jnp.exp(inter - m_row)
    s_qk = lax.dot_general(qb, kb, (((1,), (1,)), ((), ())), preferred_element_type=F32) * w_intra
    state = c_ref[...]
    tot = (jnp.dot(s_qk.astype(BF16), vext, preferred_element_type=F32)
           + a_inter * jnp.dot(qb, state.astype(BF16), preferred_element_type=F32))
    num = tot[:, :M_DV]
    den = tot[:, M_DV:M_DV + 1]
    h = num / jnp.maximum(jnp.abs(den), jnp.exp(-m_row))
    h = h * lax.rsqrt(jnp.mean(h * h, axis=-1, keepdims=True) + EPS)
    o_ref[...] = h.astype(o_ref.dtype)

    g_last = g_col[n - 1:n, :]
    w_k = g_last - g_col + i_col
    m_new = jnp.maximum(g_last + m, jnp.max(w_k, axis=0, keepdims=True))
    decay = jnp.exp(g_last + m - m_new)
    kw = (k * jnp.exp(w_k - m_new)).astype(BF16)
    upd = lax.dot_general(kw, vext, (((0,), (0,)), ((), ())), preferred_element_type=F32)
    c_ref[...] = decay * state + upd
    m_ref[...] = m_new


def _mlstm(p, conv_w, conv_b, g_rows, g_cols, chunk):
    s = p.shape[0]
    nc = s // chunk
    kq, kv = P_MK // M_DQK, P_MV // M_DV
    return pl.pallas_call(
        _mlstm_body,
        out_shape=jax.ShapeDtypeStruct((s, M_V_W), BF16),
        grid=(M_HEADS, nc),
        in_specs=[pl.BlockSpec((chunk, M_DQK), lambda h, c: (c, h)),
                  pl.BlockSpec((chunk, M_DQK), lambda h, c: (c, kq + h)),
                  pl.BlockSpec((chunk, M_DV), lambda h, c: (c, kv + h)),
                  pl.BlockSpec((CONV_K, M_DQK), lambda h, c: (0, h)),
                  pl.BlockSpec((CONV_K, M_DQK), lambda h, c: (0, kq + h)),
                  pl.BlockSpec((1, M_DQK), lambda h, c: (0, h)),
                  pl.BlockSpec((1, M_DQK), lambda h, c: (0, kq + h)),
                  pl.BlockSpec((1, 2, chunk), lambda h, c: (h, 0, c)),
                  pl.BlockSpec((1, chunk, 2), lambda h, c: (h, c, 0))],
        out_specs=pl.BlockSpec((chunk, M_DV), lambda h, c: (c, h)),
        scratch_shapes=[pltpu.VMEM((M_DQK, M_DV + LANES), F32), pltpu.VMEM((1, 1), F32),
                        pltpu.VMEM((SUBLANES, M_DQK), F32), pltpu.VMEM((SUBLANES, M_DQK), F32)],
        compiler_params=_cparams(("arbitrary", "arbitrary")),
        name="mlstm",
    )(p, p, p, conv_w, conv_w, conv_b, conv_b, g_rows, g_cols)


SB_UNDERFLOW = 105.0
SB_HEADS_PER_STEP = 2


def _sb_body(q_ref, k_ref, v_ref, o_ref, *, tile):
    i = pl.program_id(1)
    rows = lax.broadcasted_iota(jnp.int32, (tile, tile), 0)
    cols = lax.broadcasted_iota(jnp.int32, (tile, tile), 1)
    later = (rows > cols).astype(BF16)
    diag_mask = cols < rows

    def head(hh, with_prev):
        lanes = slice(hh * SB_DH, (hh + 1) * SB_DH)
        q = (q_ref[:, lanes].astype(F32) * (SB_DH ** -0.5)).astype(BF16)

        def logits(j, masked):
            kb = k_ref[pl.ds(pl.multiple_of(j * tile, tile), tile), lanes]
            z = lax.dot_general(q, kb, (((1,), (1,)), ((), ())), preferred_element_type=F32)
            sp = jnp.maximum(z, 0.0) + jnp.log(1.0 + jnp.exp(-jnp.abs(z)))
            if masked:
                sp = jnp.where(diag_mask, sp, 0.0)
            within = jnp.dot(sp.astype(BF16), later, preferred_element_type=F32)
            return z, sp, within, jnp.sum(sp, axis=-1, keepdims=True)

        def weighted(j, z, sp, within, rsum, masked):
            vb = v_ref[pl.ds(pl.multiple_of(j * tile, tile), tile), lanes]
            a = jnp.exp((z - sp) - within - rsum)
            if masked:
                a = jnp.where(diag_mask, a, 0.0)
            return jnp.dot(a.astype(BF16), vb, preferred_element_type=F32)

        zd, spd, wd, sd = logits(i, True)
        if not with_prev:
            return weighted(i, zd, spd, wd, jnp.zeros((tile, 1), F32), True), None, None
        zp, spp, wp, sprev = logits(i - 1, False)
        acc = (weighted(i, zd, spd, wd, jnp.zeros((tile, 1), F32), True)
               + weighted(i - 1, zp, spp, wp, sd, False))

        def cond(c):
            return jnp.logical_and(c[0] >= 0, jnp.min(c[1]) <= SB_UNDERFLOW)

        def body(c):
            j, rs, ac = c
            z, sp, w, sj = logits(j, False)
            return j - 1, rs + sj, ac + weighted(j, z, sp, w, rs, False)

        def scan_earlier(rs, ac):
            return lax.while_loop(cond, body, (i - 2, rs, ac))[2]

        return acc, sd + sprev, scan_earlier

    def store(hh, acc):
        o_ref[:, hh * SB_DH:(hh + 1) * SB_DH] = acc.astype(o_ref.dtype)

    @pl.when(i == 0)
    def _():
        for hh in range(SB_HEADS_PER_STEP):
            store(hh, head(hh, False)[0])

    @pl.when(i > 0)
    def _():
        firsts = [head(hh, True) for hh in range(SB_HEADS_PER_STEP)]
        for hh, (acc, rsum, scan_earlier) in enumerate(firsts):
            store(hh, scan_earlier(rsum, acc))


def _stick_breaking(p, tile):
    s = p.shape[0]
    w = SB_HEADS_PER_STEP * SB_DH
    cq, ck, cv = P_SQ // w, P_SK // w, P_SV // w
    return pl.pallas_call(
        functools.partial(_sb_body, tile=tile),
        out_shape=jax.ShapeDtypeStruct((s, SB_W), BF16),
        grid=(SB_HEADS // SB_HEADS_PER_STEP, s // tile),
        in_specs=[pl.BlockSpec((tile, w), lambda h, i: (i, cq + h)),
                  pl.BlockSpec((s, w), lambda h, i: (0, ck + h)),
                  pl.BlockSpec((s, w), lambda h, i: (0, cv + h))],
        out_specs=pl.BlockSpec((tile, w), lambda h, i: (i, h)),
        compiler_params=_cparams(("arbitrary", "arbitrary")),
        name="stick_breaking",
    )(p, p, p)


def _out_proj_body(mo_ref, gm_ref, gs_ref, hm_ref, hs_ref, ng_ref, x_ref, g1_ref, w_ref, o_ref):
    sig = jax.nn.sigmoid
    hm = sig(mo_ref[...].astype(F32)) * (hm_ref[...].astype(F32) * ng_ref[...])
    y = sig(gm_ref[...].astype(F32)) * hm + sig(gs_ref[...].astype(F32)) * hs_ref[...].astype(F32)
    o_ref[...] = x_ref[...] + g1_ref[...] * jnp.dot(y.astype(BF16), w_ref[...], preferred_element_type=F32)


def _out_proj(p, hm, hs, norm_g, x, g1, w_out):
    s, d = x.shape
    tm = min(256, s)
    row = lambda blk: pl.BlockSpec((tm, d), lambda i: (i, blk))
    vec = pl.BlockSpec((1, d), lambda i: (0, 0))
    return pl.pallas_call(
        _out_proj_body,
        out_shape=jax.ShapeDtypeStruct((s, d), F32),
        grid=(s // tm,),
        in_specs=[row(P_MO // d), row(P_GM // d), row(P_GS // d), row(0), row(0), vec, row(0), vec,
                  pl.BlockSpec((d, d), lambda i: (0, 0))],
        out_specs=row(0),
        compiler_params=_cparams(("arbitrary",)),
        name="out_proj",
    )(p, p, p, hm, hs, norm_g, x, g1, w_out)


def _router_body(x_ref, g_ref, sc_ref, sh_ref, rwt_ref, rb_ref, h_ref, eid_ref, wt_ref):
    h = _norm_mod(x_ref[...], g_ref[...], sc_ref[...], sh_ref[...])
    h_ref[...] = h
    logits = lax.dot_general(rwt_ref[...], h, (((1,), (1,)), ((), ())),
                             precision=lax.Precision.HIGHEST, preferred_element_type=F32)
    scores = jax.nn.sigmoid(logits)
    sel = scores + rb_ref[...]
    t = sel.shape[1]
    idx = lax.broadcasted_iota(jnp.int32, (EXPERTS_PER_GROUP, t), 0)
    best = None
    for g in range(N_GROUPS):
        sl = slice(g * EXPERTS_PER_GROUP, (g + 1) * EXPERTS_PER_GROUP)
        s, sc = sel[sl], scores[sl]
        m1 = jnp.max(s, axis=0, keepdims=True)
        i1 = jnp.min(jnp.where(s == m1, idx, EXPERTS_PER_GROUP), axis=0, keepdims=True)
        s2 = jnp.where(idx == i1, -jnp.inf, s)
        m2 = jnp.max(s2, axis=0, keepdims=True)
        i2 = jnp.min(jnp.where(s2 == m2, idx, EXPERTS_PER_GROUP), axis=0, keepdims=True)
        w1 = jnp.sum(jnp.where(idx == i1, sc, 0.0), axis=0, keepdims=True)
        w2 = jnp.sum(jnp.where(idx == i2, sc, 0.0), axis=0, keepdims=True)
        cand = (m1 + m2, i1 + g * EXPERTS_PER_GROUP, i2 + g * EXPERTS_PER_GROUP, w1, w2)
        if best is None:
            best = cand
        else:
            better = cand[0] > best[0]
            best = tuple(jnp.where(better, c, b) for c, b in zip(cand, best))
    _, e1, e2, w1, w2 = best
    wsum = w1 + w2
    eid_ref[...] = jnp.concatenate([e1, e2], axis=0)
    wt_ref[...] = jnp.concatenate([w1 / wsum, w2 / wsum], axis=0)


def _router(x, g, sc, sh, router_wt, router_bias):
    s, d = x.shape
    tm = min(512, s)
    vec = pl.BlockSpec((1, d), lambda i: (0, 0))
    return pl.pallas_call(
        _router_body,
        out_shape=(jax.ShapeDtypeStruct((s, d), F32), jax.ShapeDtypeStruct((TOP_K, s), jnp.int32),
                   jax.ShapeDtypeStruct((TOP_K, s), F32)),
        grid=(s // tm,),
        in_specs=[pl.BlockSpec((tm, d), lambda i: (i, 0)), vec, vec, vec,
                  pl.BlockSpec((N_EXPERTS, d), lambda i: (0, 0)),
                  pl.BlockSpec((N_EXPERTS, 1), lambda i: (0, 0))],
        out_specs=(pl.BlockSpec((tm, d), lambda i: (i, 0)), pl.BlockSpec((TOP_K, tm), lambda i: (0, i)),
                   pl.BlockSpec((TOP_K, tm), lambda i: (0, i))),
        compiler_params=_cparams(("arbitrary",)),
        name="router",
    )(x, g, sc, sh, router_wt, router_bias.reshape(N_EXPERTS, 1))


def _dispatch(eid, bm):
    s = eid.shape[1]
    n_assign = TOP_K * s
    flat_e = eid.reshape(-1)
    onehot = (flat_e[:, None] == jnp.arange(N_EXPERTS, dtype=jnp.int32)[None, :]).astype(jnp.int32)
    csum = jnp.cumsum(onehot, axis=0)
    rank = jnp.sum(csum * onehot, axis=1) - 1
    counts = csum[-1]
    padded = (counts + bm - 1) // bm * bm
    pends = jnp.cumsum(padded)
    dest = (pends - padded)[flat_e] + rank
    nb_max = pl.cdiv(n_assign, bm) + N_EXPERTS
    flat_t = jnp.tile(jnp.arange(s, dtype=jnp.int32), TOP_K)
    slot_tok = jnp.zeros((nb_max * bm,), jnp.int32).at[dest].set(flat_t)
    nb_used = (pends[-1] // bm).astype(jnp.int32).reshape(1)
    block_start = jnp.arange(nb_max, dtype=jnp.int32) * bm
    block_e = jnp.minimum(jnp.sum((pends[None, :] <= block_start[:, None]).astype(jnp.int32), axis=1),
                          N_EXPERTS - 1)
    return dest.astype(jnp.int32), slot_tok, block_e, nb_used, nb_max


def _expert_body(be_ref, st_ref, nu_ref, h_hbm, wg_ref, wu_ref, wd_ref, y_ref, xg_ref, xb_ref, sem, *, bm):
    b = pl.program_id(0)
    f = pl.program_id(1)
    nb = nu_ref[0]

    def row_copy(blk, r, slot):
        tok = st_ref[blk * bm + r]
        return pltpu.make_async_copy(h_hbm.at[pl.ds(tok, 1), :], xg_ref.at[slot, pl.ds(r, 1), :], sem.at[slot])

    def start_gather(blk, slot):
        def go(r, c):
            row_copy(blk, r, slot).start()
            return c
        lax.fori_loop(0, bm, go, 0)

    def wait_gather(blk, slot):
        def go(r, c):
            row_copy(blk, r, slot).wait()
            return c
        lax.fori_loop(0, bm, go, 0)

    @pl.when(b < nb)
    def _():
        slot = b % 2

        @pl.when(f == 0)
        def _():
            @pl.when(b == 0)
            def _():
                start_gather(0, 0)

            @pl.when(b + 1 < nb)
            def _():
                start_gather(b + 1, 1 - slot)

            wait_gather(b, slot)
            xb_ref[...] = xg_ref[slot].astype(BF16)

        x = xb_ref[...]
        gate = jnp.dot(x, wg_ref[0, 0].astype(BF16), preferred_element_type=F32)
        up = jnp.dot(x, wu_ref[0, 0].astype(BF16), preferred_element_type=F32)
        mid = (gate * jax.nn.sigmoid(gate) * up).astype(BF16)
        y = jnp.dot(mid, wd_ref[0, 0].astype(BF16), preferred_element_type=F32)

        @pl.when(f == 0)
        def _():
            y_ref[...] = y

        @pl.when(f != 0)
        def _():
            y_ref[...] += y

    @pl.when((b >= nb) & (f == 0))
    def _():
        y_ref[...] = jnp.zeros_like(y_ref)


def _experts(h, slot_tok, block_e, nb_used, nb_max, layer, w_gate, w_up, w_down, bm):
    d = h.shape[1]
    fc = MOE_FF_CHUNK
    nf = D_FF_EXPERT // fc

    def blk(b, nu):
        return jnp.minimum(b, nu[0] - 1)

    def ffc(b, f, nu):
        return jnp.where(b < nu[0], f, nf - 1)

    grid_spec = pltpu.PrefetchScalarGridSpec(
        num_scalar_prefetch=3,
        grid=(nb_max, nf),
        in_specs=[pl.BlockSpec(memory_space=pl.ANY),
                  pl.BlockSpec((1, 1, d, fc), lambda b, f, be, st, nu: (layer, be[blk(b, nu)], 0, ffc(b, f, nu))),
                  pl.BlockSpec((1, 1, d, fc), lambda b, f, be, st, nu: (layer, be[blk(b, nu)], 0, ffc(b, f, nu))),
                  pl.BlockSpec((1, 1, fc, d), lambda b, f, be, st, nu: (layer, be[blk(b, nu)], ffc(b, f, nu), 0))],
        out_specs=pl.BlockSpec((bm, d), lambda b, f, be, st, nu: (b, 0)),
        scratch_shapes=[pltpu.VMEM((2, bm, d), F32), pltpu.VMEM((bm, d), BF16), pltpu.SemaphoreType.DMA((2,))],
    )
    return pl.pallas_call(
        functools.partial(_expert_body, bm=bm),
        out_shape=jax.ShapeDtypeStruct((nb_max * bm, d), F32),
        grid_spec=grid_spec,
        compiler_params=_cparams(("arbitrary", "arbitrary")),
        name="experts",
    )(block_e, slot_tok, nb_used, h, w_gate, w_up, w_down)


def _combine_body(pos_ref, y_hbm, w_ref, x_ref, g2_ref, o_ref, yb_ref, sem, *, tc, s):
    i = pl.program_id(0)
    n = pl.num_programs(0)

    def row_copy(blk, r, k, slot):
        src = pos_ref[k * s + blk * tc + r]
        return pltpu.make_async_copy(y_hbm.at[pl.ds(src, 1), :], yb_ref.at[slot, k, pl.ds(r, 1), :], sem.at[slot])

    def start_gather(blk, slot):
        def go(r, c):
            for k in range(TOP_K):
                row_copy(blk, r, k, slot).start()
            return c
        lax.fori_loop(0, tc, go, 0)

    def wait_gather(blk, slot):
        def go(r, c):
            for k in range(TOP_K):
                row_copy(blk, r, k, slot).wait()
            return c
        lax.fori_loop(0, tc, go, 0)

    slot = i % 2

    @pl.when(i == 0)
    def _():
        start_gather(0, 0)

    @pl.when(i + 1 < n)
    def _():
        start_gather(i + 1, 1 - slot)

    wait_gather(i, slot)
    w = w_ref[...]
    moe = w[:, 0:1] * yb_ref[slot, 0] + w[:, 1:2] * yb_ref[slot, 1]
    o_ref[...] = x_ref[...] + g2_ref[...] * moe


def _combine(ys, pos, wts, x, g2, tc):
    s, d = x.shape
    grid_spec = pltpu.PrefetchScalarGridSpec(
        num_scalar_prefetch=1,
        grid=(s // tc,),
        in_specs=[pl.BlockSpec(memory_space=pl.ANY),
                  pl.BlockSpec((tc, TOP_K), lambda i, pos: (i, 0)),
                  pl.BlockSpec((tc, d), lambda i, pos: (i, 0)),
                  pl.BlockSpec((1, d), lambda i, pos: (0, 0))],
        out_specs=pl.BlockSpec((tc, d), lambda i, pos: (i, 0)),
        scratch_shapes=[pltpu.VMEM((2, TOP_K, tc, d), F32), pltpu.SemaphoreType.DMA((2,))],
    )
    return pl.pallas_call(
        functools.partial(_combine_body, tc=tc, s=s),
        out_shape=jax.ShapeDtypeStruct((s, d), F32),
        grid_spec=grid_spec,
        compiler_params=_cparams(("arbitrary",)),
        name="moe_combine",
    )(pos, ys, wts, x, g2)


def kernel(x, c, ada_w, ada_b, norm1_g, w_in, m_conv_w, m_conv_b, m_igate_b, m_fgate_b, m_norm_g, w_out,
           norm2_g, router_w, router_bias, w_gate, w_up, w_down, final_g):
    batch, s, d = x.shape
    assert batch == 1 and d == D_MODEL
    chunk = min(MLSTM_CHUNK, s)
    sb_tile = min(SB_TILE, s)
    tc = min(COMBINE_TC, s)
    xs = x.reshape(s, d)
    mod = _ada(c, ada_w, ada_b)
    router_wt = router_w.T
    for l in range(DEPTH):
        sh1, sc1, g1, sh2, sc2, g2 = [mod[l, :, i * d:(i + 1) * d] for i in range(6)]
        w_cat = jnp.concatenate([w_in[l, :, :GATE_COL0], w_in[l, :, GATE_COL0 + GATE_COLS:]],
                                axis=1).astype(BF16)
        w_if = jnp.pad(w_in[l, :, GATE_COL0:GATE_COL0 + GATE_COLS],
                       ((0, 0), (0, LANES - GATE_COLS))).astype(BF16)
        p, gates = _in_proj(xs, norm1_g[l].reshape(1, d), sc1, sh1, w_cat, w_if)
        gate_bias = jnp.concatenate([m_igate_b[l], m_fgate_b[l]]).reshape(GATE_COLS, 1)
        gproc = _gates(gates[:, :GATE_COLS].T, gate_bias, chunk)
        g_rows = gproc.reshape(2, M_HEADS, s).transpose(1, 0, 2)
        g_cols = g_rows.transpose(0, 2, 1)
        hm = _mlstm(p, m_conv_w[l], m_conv_b[l].reshape(1, -1), g_rows, g_cols, chunk)
        hs = _stick_breaking(p, sb_tile)
        xs = _out_proj(p, hm, hs, m_norm_g[l].reshape(1, d), xs, g1, w_out[l].astype(BF16))
        h2, eid, wts = _router(xs, norm2_g[l].reshape(1, d), sc2, sh2, router_wt, router_bias)
        dest, slot_tok, block_e, nb_used, nb_max = _dispatch(eid, MOE_BM)
        ys = _experts(h2, slot_tok, block_e, nb_used, nb_max, l, w_gate, w_up, w_down, MOE_BM)
        xs = _combine(ys, dest, wts.T, xs, g2, tc)
    return _final_norm(xs, final_g).reshape(batch, s, d)
```

```python
import functools

import jax
import jax.numpy as jnp
from jax import lax
from jax.experimental import pallas as pl
from jax.experimental.pallas import tpu as pltpu

F32 = jnp.float32
BF16 = jnp.bfloat16

D_MODEL = 2048
DEPTH = 2
M_HEADS = 4
M_DV = D_MODEL // M_HEADS
M_DQK = M_DV // 2
CONV_K = 4
SB_HEADS = 16
SB_DH = D_MODEL // SB_HEADS
N_EXPERTS = 32
N_GROUPS = 4
EXPERTS_PER_GROUP = N_EXPERTS // N_GROUPS
TOP_K = 2
D_FF_EXPERT = 768
EPS = 1e-6
NEG = -1e30

M_QK_W = M_HEADS * M_DQK
M_V_W = M_HEADS * M_DV
SB_W = SB_HEADS * SB_DH
GATE_COL0 = 2 * M_QK_W + 2 * M_V_W
GATE_COLS = 2 * M_HEADS
P_MQ, P_MK, P_MV, P_MO = 0, M_QK_W, 2 * M_QK_W, 2 * M_QK_W + M_V_W
P_SQ = GATE_COL0
P_SK, P_SV, P_GM, P_GS = P_SQ + SB_W, P_SQ + 2 * SB_W, P_SQ + 3 * SB_W, P_SQ + 4 * SB_W
P_W = P_SQ + 5 * SB_W

LANES = 128
SUBLANES = 8
VMEM_LIMIT = 56 * 1024 * 1024

MLSTM_CHUNK = 256
SB_TILE = 256
MOE_BM = 256
MOE_FF_CHUNK = 256
MOE_GATHER_UNROLL = 8
MOE_VMEM_LIMIT = 60 * 1024 * 1024
COMBINE_TC = 256


def _cparams(sem):
    return pltpu.CompilerParams(dimension_semantics=sem, vmem_limit_bytes=VMEM_LIMIT)


def _ada_body(c_ref, w_ref, b_ref, o_ref):
    c = c_ref[...]
    cs = c * jax.nn.sigmoid(c)
    for j in range(w_ref.shape[2] // LANES):
        sl = slice(j * LANES, (j + 1) * LANES)
        o_ref[0, :, sl] = jnp.sum(w_ref[0, :, sl] * cs, axis=0, keepdims=True) + b_ref[0, :, sl]


def _ada(c, ada_w, ada_b):
    depth, d, n = ada_w.shape
    tn = 1024
    cb = jnp.broadcast_to(c.reshape(d, 1), (d, LANES))
    return pl.pallas_call(
        _ada_body,
        out_shape=jax.ShapeDtypeStruct((depth, 1, n), F32),
        grid=(depth, n // tn),
        in_specs=[pl.BlockSpec((d, LANES), lambda l, j: (0, 0)),
                  pl.BlockSpec((1, d, tn), lambda l, j: (l, 0, j)),
                  pl.BlockSpec((1, 1, tn), lambda l, j: (l, 0, j))],
        out_specs=pl.BlockSpec((1, 1, tn), lambda l, j: (l, 0, j)),
        compiler_params=_cparams(("arbitrary", "arbitrary")),
        name="ada_mod",
    )(cb, ada_w, ada_b.reshape(depth, 1, n))


def _norm_mod(x, g, sc, sh):
    y = x * lax.rsqrt(jnp.mean(x * x, axis=-1, keepdims=True) + EPS)
    return (y * g) * (1.0 + sc) + sh


def _final_norm_body(x_ref, g_ref, o_ref):
    x = x_ref[...]
    o_ref[...] = (x * lax.rsqrt(jnp.mean(x * x, axis=-1, keepdims=True) + EPS)) * g_ref[...]


def _final_norm(x, g):
    s, d = x.shape
    tm = min(512, s)
    return pl.pallas_call(
        _final_norm_body,
        out_shape=jax.ShapeDtypeStruct((s, d), F32),
        grid=(s // tm,),
        in_specs=[pl.BlockSpec((tm, d), lambda i: (i, 0)), pl.BlockSpec((1, d), lambda i: (0, 0))],
        out_specs=pl.BlockSpec((tm, d), lambda i: (i, 0)),
        compiler_params=_cparams(("arbitrary",)),
        name="final_norm",
    )(x, g.reshape(1, d))


def _norm1_body(x_ref, g_ref, sc_ref, sh_ref, o_ref):
    o_ref[...] = _norm_mod(x_ref[...], g_ref[...], sc_ref[...], sh_ref[...]).astype(o_ref.dtype)


def _norm1(x, g, sc, sh):
    s, d = x.shape
    tm = min(512, s)
    vec = pl.BlockSpec((1, d), lambda i: (0, 0))
    return pl.pallas_call(
        _norm1_body,
        out_shape=jax.ShapeDtypeStruct((s, d), BF16),
        grid=(s // tm,),
        in_specs=[pl.BlockSpec((tm, d), lambda i: (i, 0)), vec, vec, vec],
        out_specs=pl.BlockSpec((tm, d), lambda i: (i, 0)),
        compiler_params=_cparams(("arbitrary",)),
        name="norm1",
    )(x, g, sc, sh)


IN_PROJ_TN = 1024


def _in_proj_body(h_ref, a_ref, b_ref, p_ref, w_ref):
    j = pl.program_id(0)

    @pl.when(pl.program_id(1) == 0)
    def _():
        @pl.when(j < GATE_COL0 // IN_PROJ_TN)
        def _():
            w_ref[...] = a_ref[0].astype(BF16)

        @pl.when(j >= GATE_COL0 // IN_PROJ_TN)
        def _():
            a = a_ref[0]
            tn = a.shape[1]
            ra = pltpu.roll(a, tn - GATE_COLS, 1)
            rb = pltpu.roll(b_ref[0], LANES - GATE_COLS, 1)
            lane = lax.broadcasted_iota(jnp.int32, rb.shape, 1)
            w_ref[:, :tn - LANES] = ra[:, :tn - LANES].astype(BF16)
            w_ref[:, tn - LANES:] = jnp.where(lane >= LANES - GATE_COLS, rb, ra[:, tn - LANES:]).astype(BF16)

    p_ref[...] = jnp.dot(h_ref[...], w_ref[...], preferred_element_type=F32).astype(p_ref.dtype)


def _in_proj(h, w_in, layer):
    s, d = h.shape
    tm, tn = min(1024, s), IN_PROJ_TN
    return pl.pallas_call(
        _in_proj_body,
        out_shape=jax.ShapeDtypeStruct((s, P_W), BF16),
        grid=(P_W // tn, s // tm),
        in_specs=[pl.BlockSpec((tm, d), lambda j, i: (i, 0)),
                  pl.BlockSpec((1, d, tn), lambda j, i: (layer, 0, j)),
                  pl.BlockSpec((1, d, LANES), lambda j, i: (layer, 0, (j + 1) * (tn // LANES)))],
        out_specs=pl.BlockSpec((tm, tn), lambda j, i: (i, j)),
        scratch_shapes=[pltpu.VMEM((d, tn), BF16)],
        compiler_params=_cparams(("arbitrary", "arbitrary")),
        name="in_proj",
    )(h, w_in, w_in)


def _gate_proj_body(h_ref, w_ref, o_ref):
    o_ref[...] = jnp.dot(h_ref[...], w_ref[0].astype(BF16), preferred_element_type=F32)


def _gate_proj(h, w_in, layer):
    s, d = h.shape
    tm = min(1024, s)
    return pl.pallas_call(
        _gate_proj_body,
        out_shape=jax.ShapeDtypeStruct((s, LANES), F32),
        grid=(s // tm,),
        in_specs=[pl.BlockSpec((tm, d), lambda i: (i, 0)),
                  pl.BlockSpec((1, d, LANES), lambda i: (layer, 0, GATE_COL0 // LANES))],
        out_specs=pl.BlockSpec((tm, LANES), lambda i: (i, 0)),
        compiler_params=_cparams(("arbitrary",)),
        name="gate_proj",
    )(h, w_in)


def _gates_body(g_ref, b_ref, o_ref):
    v = g_ref[...] + b_ref[...]
    n = v.shape[1]
    logf = jnp.minimum(v, 0.0) - jnp.log(1.0 + jnp.exp(-jnp.abs(v)))
    upper = (lax.broadcasted_iota(jnp.int32, (n, n), 0)
             <= lax.broadcasted_iota(jnp.int32, (n, n), 1)).astype(F32)
    csum = jnp.dot(logf, upper, precision=lax.Precision.HIGHEST, preferred_element_type=F32)
    row = lax.broadcasted_iota(jnp.int32, v.shape, 0)
    o_ref[...] = jnp.where(row < M_HEADS, v, csum)


def _gates(gates_t, bias, chunk):
    r, s = gates_t.shape
    return pl.pallas_call(
        _gates_body,
        out_shape=jax.ShapeDtypeStruct((r, s), F32),
        grid=(s // chunk,),
        in_specs=[pl.BlockSpec((r, chunk), lambda c: (0, c)), pl.BlockSpec((r, 1), lambda c: (0, 0))],
        out_specs=pl.BlockSpec((r, chunk), lambda c: (0, c)),
        compiler_params=_cparams(("arbitrary",)),
        name="mlstm_gates",
    )(gates_t, bias)


def _conv_silu(x_ref, tail_ref, w_ref, b_ref):
    x = x_ref[...].astype(F32)
    n = x.shape[0]
    tail = tail_ref[...]
    w = w_ref[...]
    row8 = lax.broadcasted_iota(jnp.int32, tail.shape, 0)
    y = b_ref[...] + w[CONV_K - 1:CONV_K, :] * x
    for d in range(1, CONV_K):
        rolled = pltpu.roll(x, d, 0)
        head = jnp.where(row8 < d, pltpu.roll(tail, d, 0), rolled[:SUBLANES])
        xd = jnp.concatenate([head, rolled[SUBLANES:]], axis=0)
        y = y + w[CONV_K - 1 - d:CONV_K - d, :] * xd
    tail_ref[...] = x[n - SUBLANES:, :]
    return y * jax.nn.sigmoid(y)


def _mlstm_body(q_ref, k_ref, v_ref, cwq_ref, cwk_ref, cbq_ref, cbk_ref, gr_ref, gc_ref, o_ref,
                c_ref, m_ref, tq_ref, tk_ref):
    @pl.when(pl.program_id(1) == 0)
    def _():
        c_ref[...] = jnp.zeros_like(c_ref)
        m_ref[...] = jnp.zeros_like(m_ref)
        tq_ref[...] = jnp.zeros_like(tq_ref)
        tk_ref[...] = jnp.zeros_like(tk_ref)

    q = _conv_silu(q_ref, tq_ref, cwq_ref, cbq_ref) * (M_DQK ** -0.5)
    k = _conv_silu(k_ref, tk_ref, cwk_ref, cbk_ref)
    n = q.shape[0]
    qb = q.astype(BF16)
    kb = k.astype(BF16)
    vext = jnp.concatenate([v_ref[...], jnp.ones((n, LANES), BF16)], axis=1)

    gr = gr_ref[0]
    gc = gc_ref[0]
    i_row, g_row = gr[0:1, :], gr[1:2, :]
    i_col, g_col = gc[:, 0:1], gc[:, 1:2]
    m = m_ref[...]

    causal = (lax.broadcasted_iota(jnp.int32, (n, n), 1) <= lax.broadcasted_iota(jnp.int32, (n, n), 0))
    dmat = jnp.where(causal, g_col - g_row + i_row, NEG)
    inter = g_col + m
    m_row = jnp.maximum(inter, jnp.max(dmat, axis=-1, keepdims=True))
    w_intra = jnp.exp(dmat - m_row)
    a_inter = jnp.exp(inter - m_row)
    s_qk = lax.dot_general(qb, kb, (((1,), (1,)), ((), ())), preferred_element_type=F32) * w_intra
    state = c_ref[...]
    tot = (jnp.dot(s_qk.astype(BF16), vext, preferred_element_type=F32)
           + a_inter * jnp.dot(qb, state.astype(BF16), preferred_element_type=F32))
    num = tot[:, :M_DV]
    den = tot[:, M_DV:M_DV + 1]
    h = num / jnp.maximum(jnp.abs(den), jnp.exp(-m_row))
    h = h * lax.rsqrt(jnp.mean(h * h, axis=-1, keepdims=True) + EPS)
    o_ref[...] = h.astype(o_ref.dtype)

    g_last = g_col[n - 1:n, :]
    w_k = g_last - g_col + i_col
    m_new = jnp.maximum(g_last + m, jnp.max(w_k, axis=0, keepdims=True))
    decay = jnp.exp(g_last + m - m_new)
    kw = (k * jnp.exp(w_k - m_new)).astype(BF16)
    upd = lax.dot_general(kw, vext, (((0,), (0,)), ((), ())), preferred_element_type=F32)
    c_ref[...] = decay * state + upd
    m_ref[...] = m_new


def _mlstm(p, conv_w, conv_b, g_rows, g_cols, chunk):
    s = p.shape[0]
    nc = s // chunk
    kq, kv = P_MK // M_DQK, P_MV // M_DV
    return pl.pallas_call(
        _mlstm_body,
        out_shape=jax.ShapeDtypeStruct((s, M_V_W), BF16),
        grid=(M_HEADS, nc),
        in_specs=[pl.BlockSpec((chunk, M_DQK), lambda h, c: (c, h)),
                  pl.BlockSpec((chunk, M_DQK), lambda h, c: (c, kq + h)),
                  pl.BlockSpec((chunk, M_DV), lambda h, c: (c, kv + h)),
                  pl.BlockSpec((CONV_K, M_DQK), lambda h, c: (0, h)),
                  pl.BlockSpec((CONV_K, M_DQK), lambda h, c: (0, kq + h)),
                  pl.BlockSpec((1, M_DQK), lambda h, c: (0, h)),
                  pl.BlockSpec((1, M_DQK), lambda h, c: (0, kq + h)),
                  pl.BlockSpec((1, 2, chunk), lambda h, c: (h, 0, c)),
                  pl.BlockSpec((1, chunk, 2), lambda h, c: (h, c, 0))],
        out_specs=pl.BlockSpec((chunk, M_DV), lambda h, c: (c, h)),
        scratch_shapes=[pltpu.VMEM((M_DQK, M_DV + LANES), F32), pltpu.VMEM((1, 1), F32),
                        pltpu.VMEM((SUBLANES, M_DQK), F32), pltpu.VMEM((SUBLANES, M_DQK), F32)],
        compiler_params=_cparams(("arbitrary", "arbitrary")),
        name="mlstm",
    )(p, p, p, conv_w, conv_w, conv_b, conv_b, g_rows, g_cols)


SB_UNDERFLOW = 105.0
SB_HEADS_PER_STEP = 2


def _sb_body(q_ref, k_ref, v_ref, o_ref, *, tile):
    i = pl.program_id(1)
    rows = lax.broadcasted_iota(jnp.int32, (tile, tile), 0)
    cols = lax.broadcasted_iota(jnp.int32, (tile, tile), 1)
    later = (rows > cols).astype(BF16)
    diag_mask = cols < rows

    def head(hh, with_prev):
        lanes = slice(hh * SB_DH, (hh + 1) * SB_DH)
        q = (q_ref[:, lanes].astype(F32) * (SB_DH ** -0.5)).astype(BF16)

        def logits(j, masked):
            kb = k_ref[pl.ds(pl.multiple_of(j * tile, tile), tile), lanes]
            z = lax.dot_general(q, kb, (((1,), (1,)), ((), ())), preferred_element_type=F32)
            sp = jnp.maximum(z, 0.0) + jnp.log(1.0 + jnp.exp(-jnp.abs(z)))
            if masked:
                sp = jnp.where(diag_mask, sp, 0.0)
            within = jnp.dot(sp.astype(BF16), later, preferred_element_type=F32)
            return z, sp, within, jnp.sum(sp, axis=-1, keepdims=True)

        def weighted(j, z, sp, within, rsum, masked):
            vb = v_ref[pl.ds(pl.multiple_of(j * tile, tile), tile), lanes]
            a = jnp.exp((z - sp) - within - rsum)
            if masked:
                a = jnp.where(diag_mask, a, 0.0)
            return jnp.dot(a.astype(BF16), vb, preferred_element_type=F32)

        zd, spd, wd, sd = logits(i, True)
        if not with_prev:
            return weighted(i, zd, spd, wd, jnp.zeros((tile, 1), F32), True), None, None
        zp, spp, wp, sprev = logits(i - 1, False)
        acc = (weighted(i, zd, spd, wd, jnp.zeros((tile, 1), F32), True)
               + weighted(i - 1, zp, spp, wp, sd, False))

        def cond(c):
            return jnp.logical_and(c[0] >= 0, jnp.min(c[1]) <= SB_UNDERFLOW)

        def body(c):
            j, rs, ac = c
            z, sp, w, sj = logits(j, False)
            return j - 1, rs + sj, ac + weighted(j, z, sp, w, rs, False)

        def scan_earlier(rs, ac):
            return lax.while_loop(cond, body, (i - 2, rs, ac))[2]

        return acc, sd + sprev, scan_earlier

    def store(hh, acc):
        o_ref[:, hh * SB_DH:(hh + 1) * SB_DH] = acc.astype(o_ref.dtype)

    @pl.when(i == 0)
    def _():
        for hh in range(SB_HEADS_PER_STEP):
            store(hh, head(hh, False)[0])

    @pl.when(i > 0)
    def _():
        firsts = [head(hh, True) for hh in range(SB_HEADS_PER_STEP)]
        for hh, (acc, rsum, scan_earlier) in enumerate(firsts):
            store(hh, scan_earlier(rsum, acc))


def _stick_breaking(p, tile):
    s = p.shape[0]
    w = SB_HEADS_PER_STEP * SB_DH
    cq, ck, cv = P_SQ // w, P_SK // w, P_SV // w
    return pl.pallas_call(
        functools.partial(_sb_body, tile=tile),
        out_shape=jax.ShapeDtypeStruct((s, SB_W), BF16),
        grid=(SB_HEADS // SB_HEADS_PER_STEP, s // tile),
        in_specs=[pl.BlockSpec((tile, w), lambda h, i: (i, cq + h)),
                  pl.BlockSpec((s, w), lambda h, i: (0, ck + h)),
                  pl.BlockSpec((s, w), lambda h, i: (0, cv + h))],
        out_specs=pl.BlockSpec((tile, w), lambda h, i: (i, h)),
        compiler_params=_cparams(("arbitrary", "arbitrary")),
        name="stick_breaking",
    )(p, p, p)


def _out_proj_body(mo_ref, gm_ref, gs_ref, hm_ref, hs_ref, ng_ref, x_ref, g1_ref, w_ref, o_ref):
    sig = jax.nn.sigmoid
    hm = sig(mo_ref[...].astype(F32)) * (hm_ref[...].astype(F32) * ng_ref[...])
    y = sig(gm_ref[...].astype(F32)) * hm + sig(gs_ref[...].astype(F32)) * hs_ref[...].astype(F32)
    o_ref[...] = x_ref[...] + g1_ref[...] * jnp.dot(y.astype(BF16), w_ref[...], preferred_element_type=F32)


def _out_proj(p, hm, hs, norm_g, x, g1, w_out):
    s, d = x.shape
    tm = min(256, s)
    row = lambda blk: pl.BlockSpec((tm, d), lambda i: (i, blk))
    vec = pl.BlockSpec((1, d), lambda i: (0, 0))
    return pl.pallas_call(
        _out_proj_body,
        out_shape=jax.ShapeDtypeStruct((s, d), F32),
        grid=(s // tm,),
        in_specs=[row(P_MO // d), row(P_GM // d), row(P_GS // d), row(0), row(0), vec, row(0), vec,
                  pl.BlockSpec((d, d), lambda i: (0, 0))],
        out_specs=row(0),
        compiler_params=_cparams(("arbitrary",)),
        name="out_proj",
    )(p, p, p, hm, hs, norm_g, x, g1, w_out)


def _router_body(x_ref, g_ref, sc_ref, sh_ref, rwt_ref, rb_ref, h_ref, eid_ref, wt_ref):
    h = _norm_mod(x_ref[...], g_ref[...], sc_ref[...], sh_ref[...])
    h_ref[...] = h
    logits = lax.dot_general(rwt_ref[...], h, (((1,), (1,)), ((), ())),
                             precision=lax.Precision.HIGHEST, preferred_element_type=F32)
    scores = jax.nn.sigmoid(logits)
    sel = scores + rb_ref[...]
    t = sel.shape[1]
    idx = lax.broadcasted_iota(jnp.int32, (EXPERTS_PER_GROUP, t), 0)
    best = None
    for g in range(N_GROUPS):
        sl = slice(g * EXPERTS_PER_GROUP, (g + 1) * EXPERTS_PER_GROUP)
        s, sc = sel[sl], scores[sl]
        m1 = jnp.max(s, axis=0, keepdims=True)
        i1 = jnp.min(jnp.where(s == m1, idx, EXPERTS_PER_GROUP), axis=0, keepdims=True)
        s2 = jnp.where(idx == i1, -jnp.inf, s)
        m2 = jnp.max(s2, axis=0, keepdims=True)
        i2 = jnp.min(jnp.where(s2 == m2, idx, EXPERTS_PER_GROUP), axis=0, keepdims=True)
        w1 = jnp.sum(jnp.where(idx == i1, sc, 0.0), axis=0, keepdims=True)
        w2 = jnp.sum(jnp.where(idx == i2, sc, 0.0), axis=0, keepdims=True)
        cand = (m1 + m2, i1 + g * EXPERTS_PER_GROUP, i2 + g * EXPERTS_PER_GROUP, w1, w2)
        if best is None:
            best = cand
        else:
            better = cand[0] > best[0]
            best = tuple(jnp.where(better, c, b) for c, b in zip(cand, best))
    _, e1, e2, w1, w2 = best
    wsum = w1 + w2
    eid_ref[...] = jnp.concatenate([e1, e2], axis=0)
    wt_ref[...] = jnp.concatenate([w1 / wsum, w2 / wsum], axis=0)


def _router(x, g, sc, sh, router_wt, router_bias):
    s, d = x.shape
    tm = min(512, s)
    vec = pl.BlockSpec((1, d), lambda i: (0, 0))
    return pl.pallas_call(
        _router_body,
        out_shape=(jax.ShapeDtypeStruct((s, d), F32), jax.ShapeDtypeStruct((TOP_K, s), jnp.int32),
                   jax.ShapeDtypeStruct((TOP_K, s), F32)),
        grid=(s // tm,),
        in_specs=[pl.BlockSpec((tm, d), lambda i: (i, 0)), vec, vec, vec,
                  pl.BlockSpec((N_EXPERTS, d), lambda i: (0, 0)),
                  pl.BlockSpec((N_EXPERTS, 1), lambda i: (0, 0))],
        out_specs=(pl.BlockSpec((tm, d), lambda i: (i, 0)), pl.BlockSpec((TOP_K, tm), lambda i: (0, i)),
                   pl.BlockSpec((TOP_K, tm), lambda i: (0, i))),
        compiler_params=_cparams(("arbitrary",)),
        name="router",
    )(x, g, sc, sh, router_wt, router_bias.reshape(N_EXPERTS, 1))


def _dispatch(eid, bm):
    s = eid.shape[1]
    n_assign = TOP_K * s
    flat_e = eid.reshape(-1)
    onehot = (flat_e[:, None] == jnp.arange(N_EXPERTS, dtype=jnp.int32)[None, :]).astype(jnp.int32)
    csum = jnp.cumsum(onehot, axis=0)
    rank = jnp.sum(csum * onehot, axis=1) - 1
    counts = csum[-1]
    padded = (counts + bm - 1) // bm * bm
    pends = jnp.cumsum(padded)
    dest = (pends - padded)[flat_e] + rank
    nb_max = pl.cdiv(n_assign, bm) + N_EXPERTS
    flat_t = jnp.tile(jnp.arange(s, dtype=jnp.int32), TOP_K)
    slot_tok = jnp.zeros((nb_max * bm,), jnp.int32).at[dest].set(flat_t)
    nb_used = (pends[-1] // bm).astype(jnp.int32).reshape(1)
    block_start = jnp.arange(nb_max, dtype=jnp.int32) * bm
    block_e = jnp.minimum(jnp.sum((pends[None, :] <= block_start[:, None]).astype(jnp.int32), axis=1),
                          N_EXPERTS - 1)
    return dest.astype(jnp.int32), slot_tok, block_e, nb_used, nb_max


def _expert_body(be_ref, st_ref, nu_ref, h_hbm, wg_ref, wu_ref, wd_ref, y_ref, xg_ref, sem, *, bm):
    b = pl.program_id(0)
    nb = nu_ref[0]

    def start_gather(blk, slot):
        def go(r, c):
            tok = st_ref[blk * bm + r]
            pltpu.make_async_copy(h_hbm.at[pl.ds(tok, 1), :], xg_ref.at[slot, pl.ds(r, 1), :], sem.at[slot]).start()
            return c
        lax.fori_loop(0, bm, go, 0, unroll=MOE_GATHER_UNROLL)

    def wait_gather(slot):
        pltpu.make_async_copy(h_hbm.at[pl.ds(0, bm), :], xg_ref.at[slot], sem.at[slot]).wait()

    @pl.when(b < nb)
    def _():
        slot = b % 2

        @pl.when(b == 0)
        def _():
            start_gather(0, 0)

        @pl.when(b + 1 < nb)
        def _():
            start_gather(b + 1, 1 - slot)

        wait_gather(slot)
        x = xg_ref[slot].astype(BF16)
        y = None
        for c in range(D_FF_EXPERT // MOE_FF_CHUNK):
            cs = slice(c * MOE_FF_CHUNK, (c + 1) * MOE_FF_CHUNK)
            gate = jnp.dot(x, wg_ref[0, 0, :, cs].astype(BF16), preferred_element_type=F32)
            up = jnp.dot(x, wu_ref[0, 0, :, cs].astype(BF16), preferred_element_type=F32)
            mid = (gate * jax.nn.sigmoid(gate) * up).astype(BF16)
            part = jnp.dot(mid, wd_ref[0, 0, cs, :].astype(BF16), preferred_element_type=F32)
            y = part if y is None else y + part
        y_ref[...] = y

    @pl.when(b >= nb)
    def _():
        y_ref[...] = jnp.zeros_like(y_ref)


def _experts(h, slot_tok, block_e, nb_used, nb_max, layer, w_gate, w_up, w_down, bm):
    d = h.shape[1]
    ff = D_FF_EXPERT

    def wmap(b, be, st, nu):
        return (layer, be[jnp.minimum(b, nu[0] - 1)], 0, 0)

    grid_spec = pltpu.PrefetchScalarGridSpec(
        num_scalar_prefetch=3,
        grid=(nb_max,),
        in_specs=[pl.BlockSpec(memory_space=pl.ANY),
                  pl.BlockSpec((1, 1, d, ff), wmap),
                  pl.BlockSpec((1, 1, d, ff), wmap),
                  pl.BlockSpec((1, 1, ff, d), wmap)],
        out_specs=pl.BlockSpec((bm, d), lambda b, be, st, nu: (b, 0)),
        scratch_shapes=[pltpu.VMEM((2, bm, d), F32), pltpu.SemaphoreType.DMA((2,))],
    )
    return pl.pallas_call(
        functools.partial(_expert_body, bm=bm),
        out_shape=jax.ShapeDtypeStruct((nb_max * bm, d), F32),
        grid_spec=grid_spec,
        compiler_params=pltpu.CompilerParams(dimension_semantics=("arbitrary",),
                                             vmem_limit_bytes=MOE_VMEM_LIMIT),
        name="experts",
    )(block_e, slot_tok, nb_used, h, w_gate, w_up, w_down)


def _combine_body(pos_ref, y_hbm, w_ref, x_ref, g2_ref, o_ref, yb_ref, sem, *, tc, s):
    i = pl.program_id(0)
    n = pl.num_programs(0)

    def start_gather(blk, slot):
        def go(r, c):
            for k in range(TOP_K):
                src = pos_ref[k * s + blk * tc + r]
                pltpu.make_async_copy(y_hbm.at[pl.ds(src, 1), :], yb_ref.at[slot, k, pl.ds(r, 1), :],
                                      sem.at[slot]).start()
            return c
        lax.fori_loop(0, tc, go, 0, unroll=MOE_GATHER_UNROLL // TOP_K)

    def wait_gather(slot):
        for k in range(TOP_K):
            pltpu.make_async_copy(y_hbm.at[pl.ds(0, tc), :], yb_ref.at[slot, k], sem.at[slot]).wait()

    slot = i % 2

    @pl.when(i == 0)
    def _():
        start_gather(0, 0)

    @pl.when(i + 1 < n)
    def _():
        start_gather(i + 1, 1 - slot)

    wait_gather(slot)
    w = w_ref[...]
    moe = w[:, 0:1] * yb_ref[slot, 0] + w[:, 1:2] * yb_ref[slot, 1]
    o_ref[...] = x_ref[...] + g2_ref[...] * moe


def _combine(ys, pos, wts, x, g2, tc):
    s, d = x.shape
    grid_spec = pltpu.PrefetchScalarGridSpec(
        num_scalar_prefetch=1,
        grid=(s // tc,),
        in_specs=[pl.BlockSpec(memory_space=pl.ANY),
                  pl.BlockSpec((tc, TOP_K), lambda i, pos: (i, 0)),
                  pl.BlockSpec((tc, d), lambda i, pos: (i, 0)),
                  pl.BlockSpec((1, d), lambda i, pos: (0, 0))],
        out_specs=pl.BlockSpec((tc, d), lambda i, pos: (i, 0)),
        scratch_shapes=[pltpu.VMEM((2, TOP_K, tc, d), F32), pltpu.SemaphoreType.DMA((2,))],
    )
    return pl.pallas_call(
        functools.partial(_combine_body, tc=tc, s=s),
        out_shape=jax.ShapeDtypeStruct((s, d), F32),
        grid_spec=grid_spec,
        compiler_params=_cparams(("arbitrary",)),
        name="moe_combine",
    )(pos, ys, wts, x, g2)


def kernel(x, c, ada_w, ada_b, norm1_g, w_in, m_conv_w, m_conv_b, m_igate_b, m_fgate_b, m_norm_g, w_out,
           norm2_g, router_w, router_bias, w_gate, w_up, w_down, final_g):
    batch, s, d = x.shape
    assert batch == 1 and d == D_MODEL
    chunk = min(MLSTM_CHUNK, s)
    sb_tile = min(SB_TILE, s)
    tc = min(COMBINE_TC, s)
    xs = x.reshape(s, d)
    mod = _ada(c, ada_w, ada_b)
    router_wt = router_w.T
    for l in range(DEPTH):
        sh1, sc1, g1, sh2, sc2, g2 = [mod[l, :, i * d:(i + 1) * d] for i in range(6)]
        h1 = _norm1(xs, norm1_g[l].reshape(1, d), sc1, sh1)
        p = _in_proj(h1, w_in, l)
        gates = _gate_proj(h1, w_in, l)
        gate_bias = jnp.concatenate([m_igate_b[l], m_fgate_b[l]]).reshape(GATE_COLS, 1)
        gproc = _gates(gates[:, :GATE_COLS].T, gate_bias, chunk)
        g_rows = gproc.reshape(2, M_HEADS, s).transpose(1, 0, 2)
        g_cols = g_rows.transpose(0, 2, 1)
        hm = _mlstm(p, m_conv_w[l], m_conv_b[l].reshape(1, -1), g_rows, g_cols, chunk)
        hs = _stick_breaking(p, sb_tile)
        xs = _out_proj(p, hm, hs, m_norm_g[l].reshape(1, d), xs, g1, w_out[l].astype(BF16))
        h2, eid, wts = _router(xs, norm2_g[l].reshape(1, d), sc2, sh2, router_wt, router_bias)
        dest, slot_tok, block_e, nb_used, nb_max = _dispatch(eid, MOE_BM)
        ys = _experts(h2, slot_tok, block_e, nb_used, nb_max, l, w_gate, w_up, w_down, MOE_BM)
        xs = _combine(ys, dest, wts.T, xs, g2, tc)
    return _final_norm(xs, final_g).reshape(batch, s, d)
```

```python
import functools

import jax
import jax.numpy as jnp
from jax import lax
from jax.experimental import pallas as pl
from jax.experimental.pallas import tpu as pltpu

F32 = jnp.float32
BF16 = jnp.bfloat16

D_MODEL = 2048
DEPTH = 2
M_HEADS = 4
M_DV = D_MODEL // M_HEADS
M_DQK = M_DV // 2
CONV_K = 4
SB_HEADS = 16
SB_DH = D_MODEL // SB_HEADS
N_EXPERTS = 32
N_GROUPS = 4
EXPERTS_PER_GROUP = N_EXPERTS // N_GROUPS
TOP_K = 2
D_FF_EXPERT = 768
EPS = 1e-6
NEG = -1e30

M_QK_W = M_HEADS * M_DQK
M_V_W = M_HEADS * M_DV
SB_W = SB_HEADS * SB_DH
GATE_COL0 = 2 * M_QK_W + 2 * M_V_W
GATE_COLS = 2 * M_HEADS
P_MQ, P_MK, P_MV, P_MO = 0, M_QK_W, 2 * M_QK_W, 2 * M_QK_W + M_V_W
P_SQ = GATE_COL0
P_SK, P_SV, P_GM, P_GS = P_SQ + SB_W, P_SQ + 2 * SB_W, P_SQ + 3 * SB_W, P_SQ + 4 * SB_W
P_W = P_SQ + 5 * SB_W

LANES = 128
SUBLANES = 8
VMEM_LIMIT = 56 * 1024 * 1024

MLSTM_CHUNK = 256
SB_TILE = 256
MOE_BM = 256
MOE_FF_CHUNK = 256
MOE_GATHER_UNROLL = 8
MOE_VMEM_LIMIT = 60 * 1024 * 1024
COMBINE_TC = 256


def _cparams(sem):
    return pltpu.CompilerParams(dimension_semantics=sem, vmem_limit_bytes=VMEM_LIMIT)


def _ada_body(c_ref, w_ref, b_ref, o_ref):
    c = c_ref[...]
    cs = c * jax.nn.sigmoid(c)
    for j in range(w_ref.shape[2] // LANES):
        sl = slice(j * LANES, (j + 1) * LANES)
        o_ref[0, :, sl] = jnp.sum(w_ref[0, :, sl] * cs, axis=0, keepdims=True) + b_ref[0, :, sl]


def _ada(c, ada_w, ada_b):
    depth, d, n = ada_w.shape
    tn = 1024
    cb = jnp.broadcast_to(c.reshape(d, 1), (d, LANES))
    return pl.pallas_call(
        _ada_body,
        out_shape=jax.ShapeDtypeStruct((depth, 1, n), F32),
        grid=(depth, n // tn),
        in_specs=[pl.BlockSpec((d, LANES), lambda l, j: (0, 0)),
                  pl.BlockSpec((1, d, tn), lambda l, j: (l, 0, j)),
                  pl.BlockSpec((1, 1, tn), lambda l, j: (l, 0, j))],
        out_specs=pl.BlockSpec((1, 1, tn), lambda l, j: (l, 0, j)),
        compiler_params=_cparams(("arbitrary", "arbitrary")),
        name="ada_mod",
    )(cb, ada_w, ada_b.reshape(depth, 1, n))


def _norm_mod(x, g, sc, sh):
    y = x * lax.rsqrt(jnp.mean(x * x, axis=-1, keepdims=True) + EPS)
    return (y * g) * (1.0 + sc) + sh


def _norm1_body(x_ref, g_ref, sc_ref, sh_ref, o_ref):
    o_ref[...] = _norm_mod(x_ref[...], g_ref[...], sc_ref[...], sh_ref[...]).astype(o_ref.dtype)


def _norm1(x, g, sc, sh):
    s, d = x.shape
    tm = min(512, s)
    vec = pl.BlockSpec((1, d), lambda i: (0, 0))
    return pl.pallas_call(
        _norm1_body,
        out_shape=jax.ShapeDtypeStruct((s, d), BF16),
        grid=(s // tm,),
        in_specs=[pl.BlockSpec((tm, d), lambda i: (i, 0)), vec, vec, vec],
        out_specs=pl.BlockSpec((tm, d), lambda i: (i, 0)),
        compiler_params=_cparams(("arbitrary",)),
        name="norm1",
    )(x, g, sc, sh)


IN_PROJ_TN = 1024


def _in_proj_body(h_ref, wt_ref, p_ref, w_ref):
    @pl.when(pl.program_id(1) == 0)
    def _():
        w_ref[...] = wt_ref[0].T.astype(BF16)

    p_ref[...] = jnp.dot(h_ref[...], w_ref[...], preferred_element_type=F32).astype(p_ref.dtype)


def _in_proj(h, w_in_t, layer):
    s, d = h.shape
    tm, tn = min(1024, s), IN_PROJ_TN

    def w_rows(j, i):
        tiles = j * (tn // GATE_COLS) + jnp.where(j >= GATE_COL0 // tn, 1, 0)
        return (layer, pl.multiple_of(tiles * GATE_COLS, GATE_COLS), 0)

    return pl.pallas_call(
        _in_proj_body,
        out_shape=jax.ShapeDtypeStruct((s, P_W), BF16),
        grid=(P_W // tn, s // tm),
        in_specs=[pl.BlockSpec((tm, d), lambda j, i: (i, 0)),
                  pl.BlockSpec((pl.Element(1), pl.Element(tn), pl.Element(d)), w_rows)],
        out_specs=pl.BlockSpec((tm, tn), lambda j, i: (i, j)),
        scratch_shapes=[pltpu.VMEM((d, tn), BF16)],
        compiler_params=_cparams(("arbitrary", "arbitrary")),
        name="in_proj",
    )(h, w_in_t)


def _gates_body(h_ref, w_ref, b_ref, o_ref):
    v = lax.dot_general(w_ref[0].astype(BF16), h_ref[...], (((1,), (1,)), ((), ())),
                        preferred_element_type=F32) + b_ref[...]
    n = v.shape[1]
    logf = jnp.minimum(v, 0.0) - jnp.log(1.0 + jnp.exp(-jnp.abs(v)))
    upper = (lax.broadcasted_iota(jnp.int32, (n, n), 0)
             <= lax.broadcasted_iota(jnp.int32, (n, n), 1)).astype(F32)
    csum = jnp.dot(logf, upper, precision=lax.Precision.HIGHEST, preferred_element_type=F32)
    row = lax.broadcasted_iota(jnp.int32, v.shape, 0)
    o_ref[...] = jnp.where(row < M_HEADS, v, csum)


def _gates(h, w_in_t, bias, layer, chunk):
    s, d = h.shape
    return pl.pallas_call(
        _gates_body,
        out_shape=jax.ShapeDtypeStruct((GATE_COLS, s), F32),
        grid=(s // chunk,),
        in_specs=[pl.BlockSpec((chunk, d), lambda c: (c, 0)),
                  pl.BlockSpec((1, GATE_COLS, d), lambda c: (layer, GATE_COL0 // GATE_COLS, 0)),
                  pl.BlockSpec((GATE_COLS, 1), lambda c: (0, 0))],
        out_specs=pl.BlockSpec((GATE_COLS, chunk), lambda c: (0, c)),
        compiler_params=_cparams(("arbitrary",)),
        name="mlstm_gates",
    )(h, w_in_t, bias)


def _conv_silu(x_ref, tail_ref, w_ref, b_ref):
    x = x_ref[...].astype(F32)
    n = x.shape[0]
    tail = tail_ref[...]
    w = w_ref[...]
    row8 = lax.broadcasted_iota(jnp.int32, tail.shape, 0)
    y = b_ref[...] + w[CONV_K - 1:CONV_K, :] * x
    for d in range(1, CONV_K):
        rolled = pltpu.roll(x, d, 0)
        head = jnp.where(row8 < d, pltpu.roll(tail, d, 0), rolled[:SUBLANES])
        xd = jnp.concatenate([head, rolled[SUBLANES:]], axis=0)
        y = y + w[CONV_K - 1 - d:CONV_K - d, :] * xd
    tail_ref[...] = x[n - SUBLANES:, :]
    return y * jax.nn.sigmoid(y)


def _mlstm_body(q_ref, k_ref, v_ref, cwq_ref, cwk_ref, cbq_ref, cbk_ref, gr_ref, gc_ref, o_ref,
                c_ref, m_ref, tq_ref, tk_ref):
    @pl.when(pl.program_id(1) == 0)
    def _():
        c_ref[...] = jnp.zeros_like(c_ref)
        m_ref[...] = jnp.zeros_like(m_ref)
        tq_ref[...] = jnp.zeros_like(tq_ref)
        tk_ref[...] = jnp.zeros_like(tk_ref)

    q = _conv_silu(q_ref, tq_ref, cwq_ref, cbq_ref) * (M_DQK ** -0.5)
    k = _conv_silu(k_ref, tk_ref, cwk_ref, cbk_ref)
    n = q.shape[0]
    qb = q.astype(BF16)
    kb = k.astype(BF16)
    vext = jnp.concatenate([v_ref[...], jnp.ones((n, LANES), BF16)], axis=1)

    gr = gr_ref[0]
    gc = gc_ref[0]
    i_row, g_row = gr[0:1, :], gr[1:2, :]
    i_col, g_col = gc[:, 0:1], gc[:, 1:2]
    m = m_ref[...]

    causal = (lax.broadcasted_iota(jnp.int32, (n, n), 1) <= lax.broadcasted_iota(jnp.int32, (n, n), 0))
    dmat = jnp.where(causal, g_col - g_row + i_row, NEG)
    inter = g_col + m
    m_row = jnp.maximum(inter, jnp.max(dmat, axis=-1, keepdims=True))
    w_intra = jnp.exp(dmat - m_row)
    a_inter = jnp.exp(inter - m_row)
    s_qk = lax.dot_general(qb, kb, (((1,), (1,)), ((), ())), preferred_element_type=F32) * w_intra
    state = c_ref[...]
    tot = (jnp.dot(s_qk.astype(BF16), vext, preferred_element_type=F32)
           + a_inter * jnp.dot(qb, state.astype(BF16), preferred_element_type=F32))
    num = tot[:, :M_DV]
    den = tot[:, M_DV:M_DV + 1]
    h = num / jnp.maximum(jnp.abs(den), jnp.exp(-m_row))
    h = h * lax.rsqrt(jnp.mean(h * h, axis=-1, keepdims=True) + EPS)
    o_ref[...] = h.astype(o_ref.dtype)

    g_last = g_col[n - 1:n, :]
    w_k = g_last - g_col + i_col
    m_new = jnp.maximum(g_last + m, jnp.max(w_k, axis=0, keepdims=True))
    decay = jnp.exp(g_last + m - m_new)
    kw = (k * jnp.exp(w_k - m_new)).astype(BF16)
    upd = lax.dot_general(kw, vext, (((0,), (0,)), ((), ())), preferred_element_type=F32)
    c_ref[...] = decay * state + upd
    m_ref[...] = m_new


def _mlstm(p, conv_w, conv_b, g_rows, g_cols, chunk):
    s = p.shape[0]
    nc = s // chunk
    kq, kv = P_MK // M_DQK, P_MV // M_DV
    return pl.pallas_call(
        _mlstm_body,
        out_shape=jax.ShapeDtypeStruct((s, M_V_W), BF16),
        grid=(M_HEADS, nc),
        in_specs=[pl.BlockSpec((chunk, M_DQK), lambda h, c: (c, h)),
                  pl.BlockSpec((chunk, M_DQK), lambda h, c: (c, kq + h)),
                  pl.BlockSpec((chunk, M_DV), lambda h, c: (c, kv + h)),
                  pl.BlockSpec((CONV_K, M_DQK), lambda h, c: (0, h)),
                  pl.BlockSpec((CONV_K, M_DQK), lambda h, c: (0, kq + h)),
                  pl.BlockSpec((1, M_DQK), lambda h, c: (0, h)),
                  pl.BlockSpec((1, M_DQK), lambda h, c: (0, kq + h)),
                  pl.BlockSpec((1, 2, chunk), lambda h, c: (h, 0, c)),
                  pl.BlockSpec((1, chunk, 2), lambda h, c: (h, c, 0))],
        out_specs=pl.BlockSpec((chunk, M_DV), lambda h, c: (c, h)),
        scratch_shapes=[pltpu.VMEM((M_DQK, M_DV + LANES), F32), pltpu.VMEM((1, 1), F32),
                        pltpu.VMEM((SUBLANES, M_DQK), F32), pltpu.VMEM((SUBLANES, M_DQK), F32)],
        compiler_params=_cparams(("arbitrary", "arbitrary")),
        name="mlstm",
    )(p, p, p, conv_w, conv_w, conv_b, conv_b, g_rows, g_cols)


SB_UNDERFLOW = 105.0
SB_HEADS_PER_STEP = 2


def _sb_body(q_ref, k_ref, v_ref, o_ref, *, tile):
    i = pl.program_id(1)
    rows = lax.broadcasted_iota(jnp.int32, (tile, tile), 0)
    cols = lax.broadcasted_iota(jnp.int32, (tile, tile), 1)
    later = (rows > cols).astype(BF16)
    diag_mask = cols < rows

    def head(hh, with_prev):
        lanes = slice(hh * SB_DH, (hh + 1) * SB_DH)
        q = (q_ref[:, lanes].astype(F32) * (SB_DH ** -0.5)).astype(BF16)

        def logits(j, masked):
            kb = k_ref[pl.ds(pl.multiple_of(j * tile, tile), tile), lanes]
            z = lax.dot_general(q, kb, (((1,), (1,)), ((), ())), preferred_element_type=F32)
            sp = jnp.maximum(z, 0.0) + jnp.log(1.0 + jnp.exp(-jnp.abs(z)))
            if masked:
                sp = jnp.where(diag_mask, sp, 0.0)
            within = jnp.dot(sp.astype(BF16), later, preferred_element_type=F32)
            return z, sp, within, jnp.sum(sp, axis=-1, keepdims=True)

        def weighted(j, z, sp, within, rsum, masked):
            vb = v_ref[pl.ds(pl.multiple_of(j * tile, tile), tile), lanes]
            a = jnp.exp((z - sp) - within - rsum)
            if masked:
                a = jnp.where(diag_mask, a, 0.0)
            return jnp.dot(a.astype(BF16), vb, preferred_element_type=F32)

        zd, spd, wd, sd = logits(i, True)
        if not with_prev:
            return weighted(i, zd, spd, wd, jnp.zeros((tile, 1), F32), True), None, None
        zp, spp, wp, sprev = logits(i - 1, False)
        acc = (weighted(i, zd, spd, wd, jnp.zeros((tile, 1), F32), True)
               + weighted(i - 1, zp, spp, wp, sd, False))

        def cond(c):
            return jnp.logical_and(c[0] >= 0, jnp.min(c[1]) <= SB_UNDERFLOW)

        def body(c):
            j, rs, ac = c
            z, sp, w, sj = logits(j, False)
            return j - 1, rs + sj, ac + weighted(j, z, sp, w, rs, False)

        def scan_earlier(rs, ac):
            return lax.while_loop(cond, body, (i - 2, rs, ac))[2]

        return acc, sd + sprev, scan_earlier

    def store(hh, acc):
        o_ref[:, hh * SB_DH:(hh + 1) * SB_DH] = acc.astype(o_ref.dtype)

    @pl.when(i == 0)
    def _():
        for hh in range(SB_HEADS_PER_STEP):
            store(hh, head(hh, False)[0])

    @pl.when(i > 0)
    def _():
        firsts = [head(hh, True) for hh in range(SB_HEADS_PER_STEP)]
        for hh, (acc, rsum, scan_earlier) in enumerate(firsts):
            store(hh, scan_earlier(rsum, acc))


def _stick_breaking(p, tile):
    s = p.shape[0]
    w = SB_HEADS_PER_STEP * SB_DH
    cq, ck, cv = P_SQ // w, P_SK // w, P_SV // w
    return pl.pallas_call(
        functools.partial(_sb_body, tile=tile),
        out_shape=jax.ShapeDtypeStruct((s, SB_W), BF16),
        grid=(SB_HEADS // SB_HEADS_PER_STEP, s // tile),
        in_specs=[pl.BlockSpec((tile, w), lambda h, i: (i, cq + h)),
                  pl.BlockSpec((s, w), lambda h, i: (0, ck + h)),
                  pl.BlockSpec((s, w), lambda h, i: (0, cv + h))],
        out_specs=pl.BlockSpec((tile, w), lambda h, i: (i, h)),
        compiler_params=_cparams(("arbitrary", "arbitrary")),
        name="stick_breaking",
    )(p, p, p)


def _out_proj_body(mo_ref, gm_ref, gs_ref, hm_ref, hs_ref, ng_ref, x_ref, g1_ref, w_ref, o_ref):
    sig = jax.nn.sigmoid
    hm = sig(mo_ref[...].astype(F32)) * (hm_ref[...].astype(F32) * ng_ref[...])
    y = sig(gm_ref[...].astype(F32)) * hm + sig(gs_ref[...].astype(F32)) * hs_ref[...].astype(F32)
    o_ref[...] = x_ref[...] + g1_ref[...] * jnp.dot(y.astype(BF16), w_ref[...], preferred_element_type=F32)


def _out_proj(p, hm, hs, norm_g, x, g1, w_out):
    s, d = x.shape
    tm = min(512, s)
    row = lambda blk: pl.BlockSpec((tm, d), lambda i: (i, blk))
    vec = pl.BlockSpec((1, d), lambda i: (0, 0))
    return pl.pallas_call(
        _out_proj_body,
        out_shape=jax.ShapeDtypeStruct((s, d), F32),
        grid=(s // tm,),
        in_specs=[row(P_MO // d), row(P_GM // d), row(P_GS // d), row(0), row(0), vec, row(0), vec,
                  pl.BlockSpec((d, d), lambda i: (0, 0), pipeline_mode=pl.Buffered(1))],
        out_specs=row(0),
        compiler_params=_cparams(("arbitrary",)),
        name="out_proj",
    )(p, p, p, hm, hs, norm_g, x, g1, w_out)


def _router_body(x_ref, g_ref, sc_ref, sh_ref, rwt_ref, rb_ref, h_ref, eid_ref, wt_ref):
    h = _norm_mod(x_ref[...], g_ref[...], sc_ref[...], sh_ref[...])
    h_ref[...] = h
    logits = lax.dot_general(rwt_ref[...], h, (((1,), (1,)), ((), ())),
                             precision=lax.Precision.HIGHEST, preferred_element_type=F32)
    scores = jax.nn.sigmoid(logits)
    sel = scores + rb_ref[...]
    t = sel.shape[1]
    idx = lax.broadcasted_iota(jnp.int32, (EXPERTS_PER_GROUP, t), 0)
    best = None
    for g in range(N_GROUPS):
        sl = slice(g * EXPERTS_PER_GROUP, (g + 1) * EXPERTS_PER_GROUP)
        s, sc = sel[sl], scores[sl]
        m1 = jnp.max(s, axis=0, keepdims=True)
        i1 = jnp.min(jnp.where(s == m1, idx, EXPERTS_PER_GROUP), axis=0, keepdims=True)
        s2 = jnp.where(idx == i1, -jnp.inf, s)
        m2 = jnp.max(s2, axis=0, keepdims=True)
        i2 = jnp.min(jnp.where(s2 == m2, idx, EXPERTS_PER_GROUP), axis=0, keepdims=True)
        w1 = jnp.sum(jnp.where(idx == i1, sc, 0.0), axis=0, keepdims=True)
        w2 = jnp.sum(jnp.where(idx == i2, sc, 0.0), axis=0, keepdims=True)
        cand = (m1 + m2, i1 + g * EXPERTS_PER_GROUP, i2 + g * EXPERTS_PER_GROUP, w1, w2)
        if best is None:
            best = cand
        else:
            better = cand[0] > best[0]
            best = tuple(jnp.where(better, c, b) for c, b in zip(cand, best))
    _, e1, e2, w1, w2 = best
    wsum = w1 + w2
    eid_ref[...] = jnp.concatenate([e1, e2], axis=0)
    wt_ref[...] = jnp.concatenate([w1 / wsum, w2 / wsum], axis=0)


def _router(x, g, sc, sh, router_wt, router_bias):
    s, d = x.shape
    tm = min(512, s)
    vec = pl.BlockSpec((1, d), lambda i: (0, 0))
    return pl.pallas_call(
        _router_body,
        out_shape=(jax.ShapeDtypeStruct((s, d), F32), jax.ShapeDtypeStruct((TOP_K, s), jnp.int32),
                   jax.ShapeDtypeStruct((TOP_K, s), F32)),
        grid=(s // tm,),
        in_specs=[pl.BlockSpec((tm, d), lambda i: (i, 0)), vec, vec, vec,
                  pl.BlockSpec((N_EXPERTS, d), lambda i: (0, 0)),
                  pl.BlockSpec((N_EXPERTS, 1), lambda i: (0, 0))],
        out_specs=(pl.BlockSpec((tm, d), lambda i: (i, 0)), pl.BlockSpec((TOP_K, tm), lambda i: (0, i)),
                   pl.BlockSpec((TOP_K, tm), lambda i: (0, i))),
        compiler_params=_cparams(("arbitrary",)),
        name="router",
    )(x, g, sc, sh, router_wt, router_bias.reshape(N_EXPERTS, 1))


def _dispatch(eid, bm):
    s = eid.shape[1]
    n_assign = TOP_K * s
    flat_e = eid.reshape(-1)
    onehot = (flat_e[:, None] == jnp.arange(N_EXPERTS, dtype=jnp.int32)[None, :]).astype(jnp.int32)
    csum = jnp.cumsum(onehot, axis=0)
    rank = jnp.sum(csum * onehot, axis=1) - 1
    counts = csum[-1]
    padded = (counts + bm - 1) // bm * bm
    pends = jnp.cumsum(padded)
    dest = (pends - padded)[flat_e] + rank
    nb_max = pl.cdiv(n_assign, bm) + N_EXPERTS
    flat_t = jnp.tile(jnp.arange(s, dtype=jnp.int32), TOP_K)
    slot_tok = jnp.zeros((nb_max * bm,), jnp.int32).at[dest].set(flat_t)
    nb_used = (pends[-1] // bm).astype(jnp.int32).reshape(1)
    block_start = jnp.arange(nb_max, dtype=jnp.int32) * bm
    block_e = jnp.minimum(jnp.sum((pends[None, :] <= block_start[:, None]).astype(jnp.int32), axis=1),
                          N_EXPERTS - 1)
    return dest.astype(jnp.int32), slot_tok, block_e, nb_used, nb_max


def _expert_body(be_ref, st_ref, nu_ref, h_hbm, wg_ref, wu_ref, wd_ref, y_ref, xa_ref, xb_ref, sem, *, bm):
    b = pl.program_id(0)
    nb = nu_ref[0]
    n_chunks = D_FF_EXPERT // MOE_FF_CHUNK

    def row_copy(tok, r, buf, si):
        return pltpu.make_async_copy(h_hbm.at[pl.ds(tok, 1), :], buf.at[pl.ds(r, 1), :], sem.at[si])

    def wait_gather(buf, si):
        pltpu.make_async_copy(h_hbm.at[pl.ds(0, bm), :], buf, sem.at[si]).wait()

    @pl.when(b == 0)
    def _():
        def go(r, c):
            row_copy(st_ref[r], r, xa_ref, 0).start()
            return c
        lax.fori_loop(0, bm, go, 0, unroll=MOE_GATHER_UNROLL)

    def step(cur, cur_si, nxt, nxt_si):
        wait_gather(cur, cur_si)
        base = jnp.minimum(b + 1, nb - 1) * bm
        x = cur[...].astype(BF16)
        y = None
        for c in range(n_chunks):
            for r in range(c * bm // n_chunks, (c + 1) * bm // n_chunks):
                row_copy(st_ref[base + r], r, nxt, nxt_si).start()
            cs = slice(c * MOE_FF_CHUNK, (c + 1) * MOE_FF_CHUNK)
            gate = jnp.dot(x, wg_ref[0, 0, :, cs].astype(BF16), preferred_element_type=F32)
            up = jnp.dot(x, wu_ref[0, 0, :, cs].astype(BF16), preferred_element_type=F32)
            mid = (gate * jax.nn.sigmoid(gate) * up).astype(BF16)
            part = jnp.dot(mid, wd_ref[0, 0, cs, :].astype(BF16), preferred_element_type=F32)
            y = part if y is None else y + part
        y_ref[...] = y

        @pl.when(b == nb - 1)
        def _():
            wait_gather(nxt, nxt_si)

    @pl.when((b < nb) & (b % 2 == 0))
    def _():
        step(xa_ref, 0, xb_ref, 1)

    @pl.when((b < nb) & (b % 2 == 1))
    def _():
        step(xb_ref, 1, xa_ref, 0)

    @pl.when(b >= nb)
    def _():
        y_ref[...] = jnp.zeros_like(y_ref)


def _experts(h, slot_tok, block_e, nb_used, nb_max, layer, w_gate, w_up, w_down, bm):
    d = h.shape[1]
    ff = D_FF_EXPERT

    def wmap(b, be, st, nu):
        return (layer, be[jnp.minimum(b, nu[0] - 1)], 0, 0)

    grid_spec = pltpu.PrefetchScalarGridSpec(
        num_scalar_prefetch=3,
        grid=(nb_max,),
        in_specs=[pl.BlockSpec(memory_space=pl.ANY),
                  pl.BlockSpec((1, 1, d, ff), wmap),
                  pl.BlockSpec((1, 1, d, ff), wmap),
                  pl.BlockSpec((1, 1, ff, d), wmap)],
        out_specs=pl.BlockSpec((bm, d), lambda b, be, st, nu: (b, 0)),
        scratch_shapes=[pltpu.VMEM((bm, d), F32), pltpu.VMEM((bm, d), F32), pltpu.SemaphoreType.DMA((2,))],
    )
    return pl.pallas_call(
        functools.partial(_expert_body, bm=bm),
        out_shape=jax.ShapeDtypeStruct((nb_max * bm, d), F32),
        grid_spec=grid_spec,
        compiler_params=pltpu.CompilerParams(dimension_semantics=("arbitrary",),
                                             vmem_limit_bytes=MOE_VMEM_LIMIT),
        name="experts",
    )(block_e, slot_tok, nb_used, h, w_gate, w_up, w_down)


def _combine_body(pos_ref, y_hbm, w_ref, x_ref, g2_ref, ng_ref, *rest, tc, s, final):
    if final:
        o_ref, yb_ref, sem = rest
    else:
        nsc_ref, nsh_ref, o_ref, h_ref, yb_ref, sem = rest
    i = pl.program_id(0)
    n = pl.num_programs(0)

    def start_gather(blk, slot):
        def go(r, c):
            for k in range(TOP_K):
                src = pos_ref[k * s + blk * tc + r]
                pltpu.make_async_copy(y_hbm.at[pl.ds(src, 1), :], yb_ref.at[slot, k, pl.ds(r, 1), :],
                                      sem.at[slot]).start()
            return c
        lax.fori_loop(0, tc, go, 0, unroll=MOE_GATHER_UNROLL // TOP_K)

    def wait_gather(slot):
        for k in range(TOP_K):
            pltpu.make_async_copy(y_hbm.at[pl.ds(0, tc), :], yb_ref.at[slot, k], sem.at[slot]).wait()

    slot = i % 2

    @pl.when(i == 0)
    def _():
        start_gather(0, 0)

    @pl.when(i + 1 < n)
    def _():
        start_gather(i + 1, 1 - slot)

    wait_gather(slot)
    w = w_ref[...]
    moe = w[:, 0:1] * yb_ref[slot, 0] + w[:, 1:2] * yb_ref[slot, 1]
    xn = x_ref[...] + g2_ref[...] * moe
    if final:
        o_ref[...] = (xn * lax.rsqrt(jnp.mean(xn * xn, axis=-1, keepdims=True) + EPS)) * ng_ref[...]
    else:
        o_ref[...] = xn
        h_ref[...] = _norm_mod(xn, ng_ref[...], nsc_ref[...], nsh_ref[...]).astype(h_ref.dtype)


def _combine(ys, pos, wts, x, g2, tc, norm_g, norm_sc=None, norm_sh=None):
    s, d = x.shape
    final = norm_sc is None
    row = pl.BlockSpec((tc, d), lambda i, pos: (i, 0))
    vec = pl.BlockSpec((1, d), lambda i, pos: (0, 0))
    norm_args = (norm_g,) if final else (norm_g, norm_sc, norm_sh)
    grid_spec = pltpu.PrefetchScalarGridSpec(
        num_scalar_prefetch=1,
        grid=(s // tc,),
        in_specs=[pl.BlockSpec(memory_space=pl.ANY), pl.BlockSpec((tc, TOP_K), lambda i, pos: (i, 0)), row, vec]
        + [vec] * len(norm_args),
        out_specs=row if final else (row, row),
        scratch_shapes=[pltpu.VMEM((2, TOP_K, tc, d), F32), pltpu.SemaphoreType.DMA((2,))],
    )
    out_f32 = jax.ShapeDtypeStruct((s, d), F32)
    return pl.pallas_call(
        functools.partial(_combine_body, tc=tc, s=s, final=final),
        out_shape=out_f32 if final else (out_f32, jax.ShapeDtypeStruct((s, d), BF16)),
        grid_spec=grid_spec,
        compiler_params=_cparams(("arbitrary",)),
        name="moe_combine",
    )(pos, ys, wts, x, g2, *norm_args)


def kernel(x, c, ada_w, ada_b, norm1_g, w_in, m_conv_w, m_conv_b, m_igate_b, m_fgate_b, m_norm_g, w_out,
           norm2_g, router_w, router_bias, w_gate, w_up, w_down, final_g):
    batch, s, d = x.shape
    assert batch == 1 and d == D_MODEL
    chunk = min(MLSTM_CHUNK, s)
    sb_tile = min(SB_TILE, s)
    tc = min(COMBINE_TC, s)
    xs = x.reshape(s, d)
    mod = _ada(c, ada_w, ada_b)
    router_wt = router_w.T
    w_in_t = jnp.swapaxes(w_in, 1, 2)
    mods = [[mod[l, :, i * d:(i + 1) * d] for i in range(6)] for l in range(DEPTH)]
    h1 = _norm1(xs, norm1_g[0].reshape(1, d), mods[0][1], mods[0][0])
    for l in range(DEPTH):
        sh1, sc1, g1, sh2, sc2, g2 = mods[l]
        p = _in_proj(h1, w_in_t, l)
        gate_bias = jnp.concatenate([m_igate_b[l], m_fgate_b[l]]).reshape(GATE_COLS, 1)
        gproc = _gates(h1, w_in_t, gate_bias, l, chunk)
        g_rows = gproc.reshape(2, M_HEADS, s).transpose(1, 0, 2)
        g_cols = g_rows.transpose(0, 2, 1)
        hm = _mlstm(p, m_conv_w[l], m_conv_b[l].reshape(1, -1), g_rows, g_cols, chunk)
        hs = _stick_breaking(p, sb_tile)
        xs = _out_proj(p, hm, hs, m_norm_g[l].reshape(1, d), xs, g1, w_out[l].astype(BF16))
        h2, eid, wts = _router(xs, norm2_g[l].reshape(1, d), sc2, sh2, router_wt, router_bias)
        dest, slot_tok, block_e, nb_used, nb_max = _dispatch(eid, MOE_BM)
        ys = _experts(h2, slot_tok, block_e, nb_used, nb_max, l, w_gate, w_up, w_down, MOE_BM)
        if l + 1 < DEPTH:
            xs, h1 = _combine(ys, dest, wts.T, xs, g2, tc, norm1_g[l + 1].reshape(1, d),
                              mods[l + 1][1], mods[l + 1][0])
        else:
            out = _combine(ys, dest, wts.T, xs, g2, tc, final_g.reshape(1, d))
    return out.reshape(batch, s, d)
```

```python
import functools

import jax
import jax.numpy as jnp
from jax import lax
from jax.experimental import pallas as pl
from jax.experimental.pallas import tpu as pltpu

F32 = jnp.float32
BF16 = jnp.bfloat16

D_MODEL = 2048
DEPTH = 2
M_HEADS = 4
M_DV = D_MODEL // M_HEADS
M_DQK = M_DV // 2
CONV_K = 4
SB_HEADS = 16
SB_DH = D_MODEL // SB_HEADS
N_EXPERTS = 32
N_GROUPS = 4
EXPERTS_PER_GROUP = N_EXPERTS // N_GROUPS
TOP_K = 2
D_FF_EXPERT = 768
EPS = 1e-6
NEG = -1e30

M_QK_W = M_HEADS * M_DQK
M_V_W = M_HEADS * M_DV
SB_W = SB_HEADS * SB_DH
GATE_COL0 = 2 * M_QK_W + 2 * M_V_W
GATE_COLS = 2 * M_HEADS
P_MQ, P_MK, P_MV, P_MO = 0, M_QK_W, 2 * M_QK_W, 2 * M_QK_W + M_V_W
P_SQ = GATE_COL0
P_SK, P_SV, P_GM, P_GS = P_SQ + SB_W, P_SQ + 2 * SB_W, P_SQ + 3 * SB_W, P_SQ + 4 * SB_W
P_W = P_SQ + 5 * SB_W

LANES = 128
SUBLANES = 8
VMEM_LIMIT = 56 * 1024 * 1024

MLSTM_CHUNK = 256
SB_TILE = 256
MOE_BM = 256
MOE_FF_CHUNK = 256
MOE_GATHER_UNROLL = 8
MOE_VMEM_LIMIT = 60 * 1024 * 1024
COMBINE_TC = 256


def _cparams(sem):
    return pltpu.CompilerParams(dimension_semantics=sem, vmem_limit_bytes=VMEM_LIMIT)


def _ada_body(c_ref, w_ref, b_ref, o_ref):
    c = c_ref[...]
    cs = c * jax.nn.sigmoid(c)
    for j in range(w_ref.shape[2] // LANES):
        sl = slice(j * LANES, (j + 1) * LANES)
        o_ref[0, :, sl] = jnp.sum(w_ref[0, :, sl] * cs, axis=0, keepdims=True) + b_ref[0, :, sl]


def _ada(c, ada_w, ada_b):
    depth, d, n = ada_w.shape
    tn = 1024
    cb = jnp.broadcast_to(c.reshape(d, 1), (d, LANES))
    return pl.pallas_call(
        _ada_body,
        out_shape=jax.ShapeDtypeStruct((depth, 1, n), F32),
        grid=(depth, n // tn),
        in_specs=[pl.BlockSpec((d, LANES), lambda l, j: (0, 0)),
                  pl.BlockSpec((1, d, tn), lambda l, j: (l, 0, j)),
                  pl.BlockSpec((1, 1, tn), lambda l, j: (l, 0, j))],
        out_specs=pl.BlockSpec((1, 1, tn), lambda l, j: (l, 0, j)),
        compiler_params=_cparams(("arbitrary", "arbitrary")),
        name="ada_mod",
    )(cb, ada_w, ada_b.reshape(depth, 1, n))


def _norm_mod(x, g, sc, sh):
    y = x * lax.rsqrt(jnp.mean(x * x, axis=-1, keepdims=True) + EPS)
    return (y * g) * (1.0 + sc) + sh


def _norm1_body(x_ref, g_ref, sc_ref, sh_ref, o_ref):
    o_ref[...] = _norm_mod(x_ref[...], g_ref[...], sc_ref[...], sh_ref[...]).astype(o_ref.dtype)


def _norm1(x, g, sc, sh):
    s, d = x.shape
    tm = min(512, s)
    vec = pl.BlockSpec((1, d), lambda i: (0, 0))
    return pl.pallas_call(
        _norm1_body,
        out_shape=jax.ShapeDtypeStruct((s, d), BF16),
        grid=(s // tm,),
        in_specs=[pl.BlockSpec((tm, d), lambda i: (i, 0)), vec, vec, vec],
        out_specs=pl.BlockSpec((tm, d), lambda i: (i, 0)),
        compiler_params=_cparams(("arbitrary",)),
        name="norm1",
    )(x, g, sc, sh)


IN_PROJ_TN = 1024


def _in_proj_body(h_ref, wt_ref, p_ref, w_ref):
    @pl.when(pl.program_id(1) == 0)
    def _():
        w_ref[...] = wt_ref[0].T.astype(BF16)

    p_ref[...] = jnp.dot(h_ref[...], w_ref[...], preferred_element_type=F32).astype(p_ref.dtype)


def _in_proj(h, w_in_t, layer):
    s, d = h.shape
    tm, tn = min(1024, s), IN_PROJ_TN

    def w_rows(j, i):
        tiles = j * (tn // GATE_COLS) + jnp.where(j >= GATE_COL0 // tn, 1, 0)
        return (layer, pl.multiple_of(tiles * GATE_COLS, GATE_COLS), 0)

    return pl.pallas_call(
        _in_proj_body,
        out_shape=jax.ShapeDtypeStruct((s, P_W), BF16),
        grid=(P_W // tn, s // tm),
        in_specs=[pl.BlockSpec((tm, d), lambda j, i: (i, 0)),
                  pl.BlockSpec((pl.Element(1), pl.Element(tn), pl.Element(d)), w_rows)],
        out_specs=pl.BlockSpec((tm, tn), lambda j, i: (i, j)),
        scratch_shapes=[pltpu.VMEM((d, tn), BF16)],
        compiler_params=_cparams(("arbitrary", "arbitrary")),
        name="in_proj",
    )(h, w_in_t)


def _gates_body(h_ref, w_ref, b_ref, o_ref):
    v = lax.dot_general(w_ref[0].astype(BF16), h_ref[...], (((1,), (1,)), ((), ())),
                        preferred_element_type=F32) + b_ref[...]
    n = v.shape[1]
    logf = jnp.minimum(v, 0.0) - jnp.log(1.0 + jnp.exp(-jnp.abs(v)))
    upper = (lax.broadcasted_iota(jnp.int32, (n, n), 0)
             <= lax.broadcasted_iota(jnp.int32, (n, n), 1)).astype(F32)
    csum = jnp.dot(logf, upper, precision=lax.Precision.HIGHEST, preferred_element_type=F32)
    row = lax.broadcasted_iota(jnp.int32, v.shape, 0)
    o_ref[...] = jnp.where(row < M_HEADS, v, csum)


def _gates(h, w_in_t, bias, layer, chunk):
    s, d = h.shape
    return pl.pallas_call(
        _gates_body,
        out_shape=jax.ShapeDtypeStruct((GATE_COLS, s), F32),
        grid=(s // chunk,),
        in_specs=[pl.BlockSpec((chunk, d), lambda c: (c, 0)),
                  pl.BlockSpec((1, GATE_COLS, d), lambda c: (layer, GATE_COL0 // GATE_COLS, 0)),
                  pl.BlockSpec((GATE_COLS, 1), lambda c: (0, 0))],
        out_specs=pl.BlockSpec((GATE_COLS, chunk), lambda c: (0, c)),
        compiler_params=_cparams(("arbitrary",)),
        name="mlstm_gates",
    )(h, w_in_t, bias)


def _conv_silu(x_ref, tail_ref, w_ref, b_ref):
    x = x_ref[...].astype(F32)
    n = x.shape[0]
    tail = tail_ref[...]
    w = w_ref[...]
    row8 = lax.broadcasted_iota(jnp.int32, tail.shape, 0)
    y = b_ref[...] + w[CONV_K - 1:CONV_K, :] * x
    for d in range(1, CONV_K):
        rolled = pltpu.roll(x, d, 0)
        head = jnp.where(row8 < d, pltpu.roll(tail, d, 0), rolled[:SUBLANES])
        xd = jnp.concatenate([head, rolled[SUBLANES:]], axis=0)
        y = y + w[CONV_K - 1 - d:CONV_K - d, :] * xd
    tail_ref[...] = x[n - SUBLANES:, :]
    return y * jax.nn.sigmoid(y)


def _mlstm_body(q_ref, k_ref, v_ref, cwq_ref, cwk_ref, cbq_ref, cbk_ref, gr_ref, gc_ref, o_ref,
                c_ref, m_ref, tq_ref, tk_ref):
    @pl.when(pl.program_id(1) == 0)
    def _():
        c_ref[...] = jnp.zeros_like(c_ref)
        m_ref[...] = jnp.zeros_like(m_ref)
        tq_ref[...] = jnp.zeros_like(tq_ref)
        tk_ref[...] = jnp.zeros_like(tk_ref)

    q = _conv_silu(q_ref, tq_ref, cwq_ref, cbq_ref) * (M_DQK ** -0.5)
    k = _conv_silu(k_ref, tk_ref, cwk_ref, cbk_ref)
    n = q.shape[0]
    qb = q.astype(BF16)
    kb = k.astype(BF16)
    vext = jnp.concatenate([v_ref[...], jnp.ones((n, LANES), BF16)], axis=1)

    gr = gr_ref[0]
    gc = gc_ref[0]
    i_row, g_row = gr[0:1, :], gr[1:2, :]
    i_col, g_col = gc[:, 0:1], gc[:, 1:2]
    m = m_ref[...]

    causal = (lax.broadcasted_iota(jnp.int32, (n, n), 1) <= lax.broadcasted_iota(jnp.int32, (n, n), 0))
    dmat = jnp.where(causal, g_col - g_row + i_row, NEG)
    inter = g_col + m
    m_row = jnp.maximum(inter, jnp.max(dmat, axis=-1, keepdims=True))
    w_intra = jnp.exp(dmat - m_row)
    a_inter = jnp.exp(inter - m_row)
    s_qk = lax.dot_general(qb, kb, (((1,), (1,)), ((), ())), preferred_element_type=F32) * w_intra
    state = c_ref[...]
    tot = (jnp.dot(s_qk.astype(BF16), vext, preferred_element_type=F32)
           + a_inter * jnp.dot(qb, state.astype(BF16), preferred_element_type=F32))
    num = tot[:, :M_DV]
    den = tot[:, M_DV:M_DV + 1]
    h = num / jnp.maximum(jnp.abs(den), jnp.exp(-m_row))
    h = h * lax.rsqrt(jnp.mean(h * h, axis=-1, keepdims=True) + EPS)
    o_ref[...] = h.astype(o_ref.dtype)

    g_last = g_col[n - 1:n, :]
    w_k = g_last - g_col + i_col
    m_new = jnp.maximum(g_last + m, jnp.max(w_k, axis=0, keepdims=True))
    decay = jnp.exp(g_last + m - m_new)
    kw = (k * jnp.exp(w_k - m_new)).astype(BF16)
    upd = lax.dot_general(kw, vext, (((0,), (0,)), ((), ())), preferred_element_type=F32)
    c_ref[...] = decay * state + upd
    m_ref[...] = m_new


def _mlstm(p, conv_w, conv_b, g_rows, g_cols, chunk):
    s = p.shape[0]
    nc = s // chunk
    kq, kv = P_MK // M_DQK, P_MV // M_DV
    return pl.pallas_call(
        _mlstm_body,
        out_shape=jax.ShapeDtypeStruct((s, M_V_W), BF16),
        grid=(M_HEADS, nc),
        in_specs=[pl.BlockSpec((chunk, M_DQK), lambda h, c: (c, h)),
                  pl.BlockSpec((chunk, M_DQK), lambda h, c: (c, kq + h)),
                  pl.BlockSpec((chunk, M_DV), lambda h, c: (c, kv + h)),
                  pl.BlockSpec((CONV_K, M_DQK), lambda h, c: (0, h)),
                  pl.BlockSpec((CONV_K, M_DQK), lambda h, c: (0, kq + h)),
                  pl.BlockSpec((1, M_DQK), lambda h, c: (0, h)),
                  pl.BlockSpec((1, M_DQK), lambda h, c: (0, kq + h)),
                  pl.BlockSpec((1, 2, chunk), lambda h, c: (h, 0, c)),
                  pl.BlockSpec((1, chunk, 2), lambda h, c: (h, c, 0))],
        out_specs=pl.BlockSpec((chunk, M_DV), lambda h, c: (c, h)),
        scratch_shapes=[pltpu.VMEM((M_DQK, M_DV + LANES), F32), pltpu.VMEM((1, 1), F32),
                        pltpu.VMEM((SUBLANES, M_DQK), F32), pltpu.VMEM((SUBLANES, M_DQK), F32)],
        compiler_params=_cparams(("arbitrary", "arbitrary")),
        name="mlstm",
    )(p, p, p, conv_w, conv_w, conv_b, conv_b, g_rows, g_cols)


SB_UNDERFLOW = 105.0
SB_HEADS_PER_STEP = 2


def _sb_body(q_ref, k_ref, v_ref, o_ref, *, tile):
    i = pl.program_id(1)
    rows = lax.broadcasted_iota(jnp.int32, (tile, tile), 0)
    cols = lax.broadcasted_iota(jnp.int32, (tile, tile), 1)
    later = (rows > cols).astype(BF16)
    diag_mask = cols < rows

    def head(hh, with_prev):
        lanes = slice(hh * SB_DH, (hh + 1) * SB_DH)
        q = (q_ref[:, lanes].astype(F32) * (SB_DH ** -0.5)).astype(BF16)

        def logits(j, masked):
            kb = k_ref[pl.ds(pl.multiple_of(j * tile, tile), tile), lanes]
            z = lax.dot_general(q, kb, (((1,), (1,)), ((), ())), preferred_element_type=F32)
            sp = jnp.maximum(z, 0.0) + jnp.log(1.0 + jnp.exp(-jnp.abs(z)))
            if masked:
                sp = jnp.where(diag_mask, sp, 0.0)
            within = jnp.dot(sp.astype(BF16), later, preferred_element_type=F32)
            return z, sp, within, jnp.sum(sp, axis=-1, keepdims=True)

        def weighted(j, z, sp, within, rsum, masked):
            vb = v_ref[pl.ds(pl.multiple_of(j * tile, tile), tile), lanes]
            a = jnp.exp((z - sp) - within - rsum)
            if masked:
                a = jnp.where(diag_mask, a, 0.0)
            return jnp.dot(a.astype(BF16), vb, preferred_element_type=F32)

        zd, spd, wd, sd = logits(i, True)
        if not with_prev:
            return weighted(i, zd, spd, wd, jnp.zeros((tile, 1), F32), True), None, None
        zp, spp, wp, sprev = logits(i - 1, False)
        acc = (weighted(i, zd, spd, wd, jnp.zeros((tile, 1), F32), True)
               + weighted(i - 1, zp, spp, wp, sd, False))

        def cond(c):
            return jnp.logical_and(c[0] >= 0, jnp.min(c[1]) <= SB_UNDERFLOW)

        def body(c):
            j, rs, ac = c
            z, sp, w, sj = logits(j, False)
            return j - 1, rs + sj, ac + weighted(j, z, sp, w, rs, False)

        def scan_earlier(rs, ac):
            return lax.while_loop(cond, body, (i - 2, rs, ac))[2]

        return acc, sd + sprev, scan_earlier

    def store(hh, acc):
        o_ref[:, hh * SB_DH:(hh + 1) * SB_DH] = acc.astype(o_ref.dtype)

    @pl.when(i == 0)
    def _():
        for hh in range(SB_HEADS_PER_STEP):
            store(hh, head(hh, False)[0])

    @pl.when(i > 0)
    def _():
        firsts = [head(hh, True) for hh in range(SB_HEADS_PER_STEP)]
        for hh, (acc, rsum, scan_earlier) in enumerate(firsts):
            store(hh, scan_earlier(rsum, acc))


def _stick_breaking(p, tile):
    s = p.shape[0]
    w = SB_HEADS_PER_STEP * SB_DH
    cq, ck, cv = P_SQ // w, P_SK // w, P_SV // w
    return pl.pallas_call(
        functools.partial(_sb_body, tile=tile),
        out_shape=jax.ShapeDtypeStruct((s, SB_W), BF16),
        grid=(SB_HEADS // SB_HEADS_PER_STEP, s // tile),
        in_specs=[pl.BlockSpec((tile, w), lambda h, i: (i, cq + h)),
                  pl.BlockSpec((s, w), lambda h, i: (0, ck + h)),
                  pl.BlockSpec((s, w), lambda h, i: (0, cv + h))],
        out_specs=pl.BlockSpec((tile, w), lambda h, i: (i, h)),
        compiler_params=_cparams(("arbitrary", "arbitrary")),
        name="stick_breaking",
    )(p, p, p)


def _out_proj_body(mo_ref, gm_ref, gs_ref, hm_ref, hs_ref, ng_ref, x_ref, g1_ref, w_ref, o_ref):
    sig = jax.nn.sigmoid
    hm = sig(mo_ref[...].astype(F32)) * (hm_ref[...].astype(F32) * ng_ref[...])
    y = sig(gm_ref[...].astype(F32)) * hm + sig(gs_ref[...].astype(F32)) * hs_ref[...].astype(F32)
    o_ref[...] = x_ref[...] + g1_ref[...] * jnp.dot(y.astype(BF16), w_ref[...], preferred_element_type=F32)


def _out_proj(p, hm, hs, norm_g, x, g1, w_out):
    s, d = x.shape
    tm = min(512, s)
    row = lambda blk: pl.BlockSpec((tm, d), lambda i: (i, blk))
    vec = pl.BlockSpec((1, d), lambda i: (0, 0))
    return pl.pallas_call(
        _out_proj_body,
        out_shape=jax.ShapeDtypeStruct((s, d), F32),
        grid=(s // tm,),
        in_specs=[row(P_MO // d), row(P_GM // d), row(P_GS // d), row(0), row(0), vec, row(0), vec,
                  pl.BlockSpec((d, d), lambda i: (0, 0), pipeline_mode=pl.Buffered(1))],
        out_specs=row(0),
        compiler_params=_cparams(("arbitrary",)),
        name="out_proj",
    )(p, p, p, hm, hs, norm_g, x, g1, w_out)


def _router_body(x_ref, g_ref, sc_ref, sh_ref, rwt_ref, rb_ref, h_ref, eid_ref, wt_ref):
    h = _norm_mod(x_ref[...], g_ref[...], sc_ref[...], sh_ref[...])
    h_ref[...] = h
    logits = lax.dot_general(rwt_ref[...], h, (((1,), (1,)), ((), ())),
                             precision=lax.Precision.HIGHEST, preferred_element_type=F32)
    scores = jax.nn.sigmoid(logits)
    sel = scores + rb_ref[...]
    t = sel.shape[1]
    idx = lax.broadcasted_iota(jnp.int32, (EXPERTS_PER_GROUP, t), 0)
    best = None
    for g in range(N_GROUPS):
        sl = slice(g * EXPERTS_PER_GROUP, (g + 1) * EXPERTS_PER_GROUP)
        s, sc = sel[sl], scores[sl]
        m1 = jnp.max(s, axis=0, keepdims=True)
        i1 = jnp.min(jnp.where(s == m1, idx, EXPERTS_PER_GROUP), axis=0, keepdims=True)
        s2 = jnp.where(idx == i1, -jnp.inf, s)
        m2 = jnp.max(s2, axis=0, keepdims=True)
        i2 = jnp.min(jnp.where(s2 == m2, idx, EXPERTS_PER_GROUP), axis=0, keepdims=True)
        w1 = jnp.sum(jnp.where(idx == i1, sc, 0.0), axis=0, keepdims=True)
        w2 = jnp.sum(jnp.where(idx == i2, sc, 0.0), axis=0, keepdims=True)
        cand = (m1 + m2, i1 + g * EXPERTS_PER_GROUP, i2 + g * EXPERTS_PER_GROUP, w1, w2)
        if best is None:
            best = cand
        else:
            better = cand[0] > best[0]
            best = tuple(jnp.where(better, c, b) for c, b in zip(cand, best))
    _, e1, e2, w1, w2 = best
    wsum = w1 + w2
    eid_ref[...] = jnp.concatenate([e1, e2], axis=0)
    wt_ref[...] = jnp.concatenate([w1 / wsum, w2 / wsum], axis=0)


def _router(x, g, sc, sh, router_wt, router_bias):
    s, d = x.shape
    tm = min(512, s)
    vec = pl.BlockSpec((1, d), lambda i: (0, 0))
    return pl.pallas_call(
        _router_body,
        out_shape=(jax.ShapeDtypeStruct((s, d), F32), jax.ShapeDtypeStruct((TOP_K, s), jnp.int32),
                   jax.ShapeDtypeStruct((TOP_K, s), F32)),
        grid=(s // tm,),
        in_specs=[pl.BlockSpec((tm, d), lambda i: (i, 0)), vec, vec, vec,
                  pl.BlockSpec((N_EXPERTS, d), lambda i: (0, 0)),
                  pl.BlockSpec((N_EXPERTS, 1), lambda i: (0, 0))],
        out_specs=(pl.BlockSpec((tm, d), lambda i: (i, 0)), pl.BlockSpec((TOP_K, tm), lambda i: (0, i)),
                   pl.BlockSpec((TOP_K, tm), lambda i: (0, i))),
        compiler_params=_cparams(("arbitrary",)),
        name="router",
    )(x, g, sc, sh, router_wt, router_bias.reshape(N_EXPERTS, 1))


def _dispatch(eid, bm):
    s = eid.shape[1]
    n_assign = TOP_K * s
    flat_e = eid.reshape(-1)
    onehot = (flat_e[:, None] == jnp.arange(N_EXPERTS, dtype=jnp.int32)[None, :]).astype(jnp.int32)
    csum = jnp.cumsum(onehot, axis=0)
    rank = jnp.sum(csum * onehot, axis=1) - 1
    counts = csum[-1]
    padded = (counts + bm - 1) // bm * bm
    pends = jnp.cumsum(padded)
    dest = (pends - padded)[flat_e] + rank
    nb_max = pl.cdiv(n_assign, bm) + N_EXPERTS
    flat_t = jnp.tile(jnp.arange(s, dtype=jnp.int32), TOP_K)
    slot_tok = jnp.zeros((nb_max * bm,), jnp.int32).at[dest].set(flat_t)
    nb_used = (pends[-1] // bm).astype(jnp.int32).reshape(1)
    block_start = jnp.arange(nb_max, dtype=jnp.int32) * bm
    block_e = jnp.minimum(jnp.sum((pends[None, :] <= block_start[:, None]).astype(jnp.int32), axis=1),
                          N_EXPERTS - 1)
    return dest.astype(jnp.int32), slot_tok, block_e, nb_used, nb_max


def _expert_schedule(block_e, nb_used):
    nb_max = block_e.shape[0]
    idx = jnp.arange(nb_max, dtype=jnp.int32)
    used = idx < nb_used[0]
    prev_e = jnp.concatenate([jnp.full((1,), -1, jnp.int32), block_e[:-1]])
    first = (used & (block_e != prev_e)).astype(jnp.int32)
    wslot = (jnp.cumsum(first) - 1) % 2
    pos = jnp.where(first == 1, idx, nb_max)
    after = jnp.concatenate([lax.cummin(pos[::-1])[::-1][1:], jnp.full((1,), nb_max, jnp.int32)])
    nxt = jnp.where(after < nb_max, block_e[jnp.minimum(after, nb_max - 1)], -1)
    return first, wslot.astype(jnp.int32), nxt.astype(jnp.int32)


def _expert_body(be_ref, first_ref, ws_ref, nxt_ref, st_ref, nu_ref, h_hbm, wg_hbm, wu_hbm, wd_hbm, y_ref,
                 xa_ref, xb_ref, wg_buf, wu_buf, wd_buf, gsem, wsem, *, bm, layer):
    b = pl.program_id(0)
    nb = nu_ref[0]
    n_chunks = D_FF_EXPERT // MOE_FF_CHUNK

    def row_copy(tok, r, buf, si):
        return pltpu.make_async_copy(h_hbm.at[pl.ds(tok, 1), :], buf.at[pl.ds(r, 1), :], gsem.at[si])

    def wait_gather(buf, si):
        pltpu.make_async_copy(h_hbm.at[pl.ds(0, bm), :], buf, gsem.at[si]).wait()

    def weight_copies(e, slot):
        return (pltpu.make_async_copy(wg_hbm.at[layer, e], wg_buf.at[slot], wsem.at[slot, 0]),
                pltpu.make_async_copy(wu_hbm.at[layer, e], wu_buf.at[slot], wsem.at[slot, 1]),
                pltpu.make_async_copy(wd_hbm.at[layer, e], wd_buf.at[slot], wsem.at[slot, 2]))

    @pl.when(b == 0)
    def _():
        for cp in weight_copies(be_ref[0], 0):
            cp.start()

        def go(r, c):
            row_copy(st_ref[r], r, xa_ref, 0).start(priority=1)
            return c
        lax.fori_loop(0, bm, go, 0, unroll=MOE_GATHER_UNROLL)

    def step(cur, cur_si, nxt_buf, nxt_si):
        slot = ws_ref[b]

        @pl.when(first_ref[b] == 1)
        def _():
            for cp in weight_copies(be_ref[b], slot):
                cp.wait()

            @pl.when(nxt_ref[b] >= 0)
            def _():
                for cp in weight_copies(nxt_ref[b], 1 - slot):
                    cp.start()

        wait_gather(cur, cur_si)
        base = jnp.minimum(b + 1, nb - 1) * bm
        x = cur[...].astype(BF16)
        y = None
        for c in range(n_chunks):
            for r in range(c * bm // n_chunks, (c + 1) * bm // n_chunks):
                row_copy(st_ref[base + r], r, nxt_buf, nxt_si).start(priority=1)
            cs = slice(c * MOE_FF_CHUNK, (c + 1) * MOE_FF_CHUNK)
            gate = jnp.dot(x, wg_buf[slot, :, cs].astype(BF16), preferred_element_type=F32)
            up = jnp.dot(x, wu_buf[slot, :, cs].astype(BF16), preferred_element_type=F32)
            mid = (gate * jax.nn.sigmoid(gate) * up).astype(BF16)
            part = jnp.dot(mid, wd_buf[slot, cs, :].astype(BF16), preferred_element_type=F32)
            y = part if y is None else y + part
        y_ref[...] = y

        @pl.when(b == nb - 1)
        def _():
            wait_gather(nxt_buf, nxt_si)

    @pl.when((b < nb) & (b % 2 == 0))
    def _():
        step(xa_ref, 0, xb_ref, 1)

    @pl.when((b < nb) & (b % 2 == 1))
    def _():
        step(xb_ref, 1, xa_ref, 0)

    @pl.when(b >= nb)
    def _():
        y_ref[...] = jnp.zeros_like(y_ref)


def _experts(h, slot_tok, block_e, nb_used, nb_max, layer, w_gate, w_up, w_down, bm):
    d = h.shape[1]
    ff = D_FF_EXPERT
    first, wslot, nxt = _expert_schedule(block_e, nb_used)
    hbm = pl.BlockSpec(memory_space=pl.ANY)
    grid_spec = pltpu.PrefetchScalarGridSpec(
        num_scalar_prefetch=6,
        grid=(nb_max,),
        in_specs=[hbm, hbm, hbm, hbm],
        out_specs=pl.BlockSpec((bm, d), lambda b, *_: (b, 0)),
        scratch_shapes=[pltpu.VMEM((bm, d), F32), pltpu.VMEM((bm, d), F32),
                        pltpu.VMEM((2, d, ff), F32), pltpu.VMEM((2, d, ff), F32), pltpu.VMEM((2, ff, d), F32),
                        pltpu.SemaphoreType.DMA((2,)), pltpu.SemaphoreType.DMA((2, 3))],
    )
    return pl.pallas_call(
        functools.partial(_expert_body, bm=bm, layer=layer),
        out_shape=jax.ShapeDtypeStruct((nb_max * bm, d), F32),
        grid_spec=grid_spec,
        compiler_params=pltpu.CompilerParams(dimension_semantics=("arbitrary",),
                                             vmem_limit_bytes=MOE_VMEM_LIMIT),
        name="experts",
    )(block_e, first, wslot, nxt, slot_tok, nb_used, h, w_gate, w_up, w_down)


def _combine_body(pos_ref, y_hbm, w_ref, x_ref, g2_ref, ng_ref, *rest, tc, s, final):
    if final:
        o_ref, yb_ref, sem = rest
    else:
        nsc_ref, nsh_ref, o_ref, h_ref, yb_ref, sem = rest
    i = pl.program_id(0)
    n = pl.num_programs(0)

    def start_gather(blk, slot):
        def go(r, c):
            for k in range(TOP_K):
                src = pos_ref[k * s + blk * tc + r]
                pltpu.make_async_copy(y_hbm.at[pl.ds(src, 1), :], yb_ref.at[slot, k, pl.ds(r, 1), :],
                                      sem.at[slot]).start()
            return c
        lax.fori_loop(0, tc, go, 0, unroll=MOE_GATHER_UNROLL // TOP_K)

    def wait_gather(slot):
        for k in range(TOP_K):
            pltpu.make_async_copy(y_hbm.at[pl.ds(0, tc), :], yb_ref.at[slot, k], sem.at[slot]).wait()

    slot = i % 2

    @pl.when(i == 0)
    def _():
        start_gather(0, 0)

    @pl.when(i + 1 < n)
    def _():
        start_gather(i + 1, 1 - slot)

    wait_gather(slot)
    w = w_ref[...]
    moe = w[:, 0:1] * yb_ref[slot, 0] + w[:, 1:2] * yb_ref[slot, 1]
    xn = x_ref[...] + g2_ref[...] * moe
    if final:
        o_ref[...] = (xn * lax.rsqrt(jnp.mean(xn * xn, axis=-1, keepdims=True) + EPS)) * ng_ref[...]
    else:
        o_ref[...] = xn
        h_ref[...] = _norm_mod(xn, ng_ref[...], nsc_ref[...], nsh_ref[...]).astype(h_ref.dtype)


def _combine(ys, pos, wts, x, g2, tc, norm_g, norm_sc=None, norm_sh=None):
    s, d = x.shape
    final = norm_sc is None
    row = pl.BlockSpec((tc, d), lambda i, pos: (i, 0))
    vec = pl.BlockSpec((1, d), lambda i, pos: (0, 0))
    norm_args = (norm_g,) if final else (norm_g, norm_sc, norm_sh)
    grid_spec = pltpu.PrefetchScalarGridSpec(
        num_scalar_prefetch=1,
        grid=(s // tc,),
        in_specs=[pl.BlockSpec(memory_space=pl.ANY), pl.BlockSpec((tc, TOP_K), lambda i, pos: (i, 0)), row, vec]
        + [vec] * len(norm_args),
        out_specs=row if final else (row, row),
        scratch_shapes=[pltpu.VMEM((2, TOP_K, tc, d), F32), pltpu.SemaphoreType.DMA((2,))],
    )
    out_f32 = jax.ShapeDtypeStruct((s, d), F32)
    return pl.pallas_call(
        functools.partial(_combine_body, tc=tc, s=s, final=final),
        out_shape=out_f32 if final else (out_f32, jax.ShapeDtypeStruct((s, d), BF16)),
        grid_spec=grid_spec,
        compiler_params=_cparams(("arbitrary",)),
        name="moe_combine",
    )(pos, ys, wts, x, g2, *norm_args)


def kernel(x, c, ada_w, ada_b, norm1_g, w_in, m_conv_w, m_conv_b, m_igate_b, m_fgate_b, m_norm_g, w_out,
           norm2_g, router_w, router_bias, w_gate, w_up, w_down, final_g):
    batch, s, d = x.shape
    assert batch == 1 and d == D_MODEL
    chunk = min(MLSTM_CHUNK, s)
    sb_tile = min(SB_TILE, s)
    tc = min(COMBINE_TC, s)
    xs = x.reshape(s, d)
    mod = _ada(c, ada_w, ada_b)
    router_wt = router_w.T
    w_in_t = jnp.swapaxes(w_in, 1, 2)
    mods = [[mod[l, :, i * d:(i + 1) * d] for i in range(6)] for l in range(DEPTH)]
    h1 = _norm1(xs, norm1_g[0].reshape(1, d), mods[0][1], mods[0][0])
    for l in range(DEPTH):
        sh1, sc1, g1, sh2, sc2, g2 = mods[l]
        p = _in_proj(h1, w_in_t, l)
        gate_bias = jnp.concatenate([m_igate_b[l], m_fgate_b[l]]).reshape(GATE_COLS, 1)
        gproc = _gates(h1, w_in_t, gate_bias, l, chunk)
        g_rows = gproc.reshape(2, M_HEADS, s).transpose(1, 0, 2)
        g_cols = g_rows.transpose(0, 2, 1)
        hm = _mlstm(p, m_conv_w[l], m_conv_b[l].reshape(1, -1), g_rows, g_cols, chunk)
        hs = _stick_breaking(p, sb_tile)
        xs = _out_proj(p, hm, hs, m_norm_g[l].reshape(1, d), xs, g1, w_out[l].astype(BF16))
        h2, eid, wts = _router(xs, norm2_g[l].reshape(1, d), sc2, sh2, router_wt, router_bias)
        dest, slot_tok, block_e, nb_used, nb_max = _dispatch(eid, MOE_BM)
        ys = _experts(h2, slot_tok, block_e, nb_used, nb_max, l, w_gate, w_up, w_down, MOE_BM)
        if l + 1 < DEPTH:
            xs, h1 = _combine(ys, dest, wts.T, xs, g2, tc, norm1_g[l + 1].reshape(1, d),
                              mods[l + 1][1], mods[l + 1][0])
        else:
            out = _combine(ys, dest, wts.T, xs, g2, tc, final_g.reshape(1, d))
    return out.reshape(batch, s, d)
```

```python
import functools

import jax
import jax.numpy as jnp
from jax import lax
from jax.experimental import pallas as pl
from jax.experimental.pallas import tpu as pltpu

F32 = jnp.float32
BF16 = jnp.bfloat16

D_MODEL = 2048
DEPTH = 2
M_HEADS = 4
M_DV = D_MODEL // M_HEADS
M_DQK = M_DV // 2
CONV_K = 4
SB_HEADS = 16
SB_DH = D_MODEL // SB_HEADS
N_EXPERTS = 32
N_GROUPS = 4
EXPERTS_PER_GROUP = N_EXPERTS // N_GROUPS
TOP_K = 2
D_FF_EXPERT = 768
EPS = 1e-6
NEG = -1e30

M_QK_W = M_HEADS * M_DQK
M_V_W = M_HEADS * M_DV
SB_W = SB_HEADS * SB_DH
GATE_COL0 = 2 * M_QK_W + 2 * M_V_W
GATE_COLS = 2 * M_HEADS
P_MQ, P_MK, P_MV, P_MO = 0, M_QK_W, 2 * M_QK_W, 2 * M_QK_W + M_V_W
P_SQ = GATE_COL0
P_SK, P_SV, P_GM, P_GS = P_SQ + SB_W, P_SQ + 2 * SB_W, P_SQ + 3 * SB_W, P_SQ + 4 * SB_W
P_W = P_SQ + 5 * SB_W

LANES = 128
SUBLANES = 8
VMEM_LIMIT = 56 * 1024 * 1024

MLSTM_CHUNK = 256
SB_TILE = 256
MOE_BM = 256
MOE_FF_CHUNK = 256
MOE_GATHER_UNROLL = 8
MOE_DISPATCH_ROWS = 512
MOE_VMEM_LIMIT = 60 * 1024 * 1024
COMBINE_TC = 256


def _cparams(sem):
    return pltpu.CompilerParams(dimension_semantics=sem, vmem_limit_bytes=VMEM_LIMIT)


def _ada_body(c_ref, w_ref, b_ref, o_ref):
    c = c_ref[...]
    cs = c * jax.nn.sigmoid(c)
    for j in range(w_ref.shape[2] // LANES):
        sl = slice(j * LANES, (j + 1) * LANES)
        o_ref[0, :, sl] = jnp.sum(w_ref[0, :, sl] * cs, axis=0, keepdims=True) + b_ref[0, :, sl]


def _ada(c, ada_w, ada_b):
    depth, d, n = ada_w.shape
    tn = 1024
    cb = jnp.broadcast_to(c.reshape(d, 1), (d, LANES))
    return pl.pallas_call(
        _ada_body,
        out_shape=jax.ShapeDtypeStruct((depth, 1, n), F32),
        grid=(depth, n // tn),
        in_specs=[pl.BlockSpec((d, LANES), lambda l, j: (0, 0)),
                  pl.BlockSpec((1, d, tn), lambda l, j: (l, 0, j)),
                  pl.BlockSpec((1, 1, tn), lambda l, j: (l, 0, j))],
        out_specs=pl.BlockSpec((1, 1, tn), lambda l, j: (l, 0, j)),
        compiler_params=_cparams(("arbitrary", "arbitrary")),
        name="ada_mod",
    )(cb, ada_w, ada_b.reshape(depth, 1, n))


def _norm_mod(x, g, sc, sh):
    y = x * lax.rsqrt(jnp.mean(x * x, axis=-1, keepdims=True) + EPS)
    return (y * g) * (1.0 + sc) + sh


def _norm1_body(x_ref, g_ref, sc_ref, sh_ref, o_ref):
    o_ref[...] = _norm_mod(x_ref[...], g_ref[...], sc_ref[...], sh_ref[...]).astype(o_ref.dtype)


def _norm1(x, g, sc, sh):
    s, d = x.shape
    tm = min(512, s)
    vec = pl.BlockSpec((1, d), lambda i: (0, 0))
    return pl.pallas_call(
        _norm1_body,
        out_shape=jax.ShapeDtypeStruct((s, d), BF16),
        grid=(s // tm,),
        in_specs=[pl.BlockSpec((tm, d), lambda i: (i, 0)), vec, vec, vec],
        out_specs=pl.BlockSpec((tm, d), lambda i: (i, 0)),
        compiler_params=_cparams(("arbitrary",)),
        name="norm1",
    )(x, g, sc, sh)


IN_PROJ_TN = 1024


def _in_proj_body(h_ref, wt_ref, p_ref, w_ref):
    @pl.when(pl.program_id(1) == 0)
    def _():
        w_ref[...] = wt_ref[0].T.astype(BF16)

    p_ref[...] = jnp.dot(h_ref[...], w_ref[...], preferred_element_type=F32).astype(p_ref.dtype)


def _in_proj(h, w_in_t, layer):
    s, d = h.shape
    tm, tn = min(1024, s), IN_PROJ_TN

    def w_rows(j, i):
        tiles = j * (tn // GATE_COLS) + jnp.where(j >= GATE_COL0 // tn, 1, 0)
        return (layer, pl.multiple_of(tiles * GATE_COLS, GATE_COLS), 0)

    return pl.pallas_call(
        _in_proj_body,
        out_shape=jax.ShapeDtypeStruct((s, P_W), BF16),
        grid=(P_W // tn, s // tm),
        in_specs=[pl.BlockSpec((tm, d), lambda j, i: (i, 0)),
                  pl.BlockSpec((pl.Element(1), pl.Element(tn), pl.Element(d)), w_rows)],
        out_specs=pl.BlockSpec((tm, tn), lambda j, i: (i, j)),
        scratch_shapes=[pltpu.VMEM((d, tn), BF16)],
        compiler_params=_cparams(("arbitrary", "arbitrary")),
        name="in_proj",
    )(h, w_in_t)


def _gates_body(h_ref, w_ref, b_ref, o_ref):
    v = lax.dot_general(w_ref[0].astype(BF16), h_ref[...], (((1,), (1,)), ((), ())),
                        preferred_element_type=F32) + b_ref[...]
    n = v.shape[1]
    logf = jnp.minimum(v, 0.0) - jnp.log(1.0 + jnp.exp(-jnp.abs(v)))
    upper = (lax.broadcasted_iota(jnp.int32, (n, n), 0)
             <= lax.broadcasted_iota(jnp.int32, (n, n), 1)).astype(F32)
    csum = jnp.dot(logf, upper, precision=lax.Precision.HIGHEST, preferred_element_type=F32)
    row = lax.broadcasted_iota(jnp.int32, v.shape, 0)
    o_ref[...] = jnp.where(row < M_HEADS, v, csum)


def _gates(h, w_in_t, bias, layer, chunk):
    s, d = h.shape
    return pl.pallas_call(
        _gates_body,
        out_shape=jax.ShapeDtypeStruct((GATE_COLS, s), F32),
        grid=(s // chunk,),
        in_specs=[pl.BlockSpec((chunk, d), lambda c: (c, 0)),
                  pl.BlockSpec((1, GATE_COLS, d), lambda c: (layer, GATE_COL0 // GATE_COLS, 0)),
                  pl.BlockSpec((GATE_COLS, 1), lambda c: (0, 0))],
        out_specs=pl.BlockSpec((GATE_COLS, chunk), lambda c: (0, c)),
        compiler_params=_cparams(("arbitrary",)),
        name="mlstm_gates",
    )(h, w_in_t, bias)


def _conv_silu(x, tail, w, b):
    row8 = lax.broadcasted_iota(jnp.int32, tail.shape, 0)
    y = b + w[CONV_K - 1:CONV_K, :] * x
    for d in range(1, CONV_K):
        rolled = pltpu.roll(x, d, 0)
        head = jnp.where(row8 < d, pltpu.roll(tail, d, 0), rolled[:SUBLANES])
        xd = jnp.concatenate([head, rolled[SUBLANES:]], axis=0)
        y = y + w[CONV_K - 1 - d:CONV_K - d, :] * xd
    return y * jax.nn.sigmoid(y)


def _mlstm_body(q_ref, k_ref, v_ref, cw_ref, cb_ref, gr_ref, gc_ref, o_ref, c_ref, m_ref, tq_ref, tk_ref):
    @pl.when(pl.program_id(0) == 0)
    def _():
        c_ref[...] = jnp.zeros_like(c_ref)
        m_ref[...] = jnp.zeros_like(m_ref)
        tq_ref[...] = jnp.zeros_like(tq_ref)
        tk_ref[...] = jnp.zeros_like(tk_ref)

    n = q_ref.shape[0]
    causal = (lax.broadcasted_iota(jnp.int32, (n, n), 1) <= lax.broadcasted_iota(jnp.int32, (n, n), 0))
    for hd in range(M_HEADS):
        qc = slice(hd * M_DQK, (hd + 1) * M_DQK)
        kc = slice(M_QK_W + hd * M_DQK, M_QK_W + (hd + 1) * M_DQK)
        vc = slice(hd * M_DV, (hd + 1) * M_DV)
        xq = q_ref[:, qc].astype(F32)
        xk = k_ref[:, qc].astype(F32)
        q = _conv_silu(xq, tq_ref[:, qc], cw_ref[:, qc], cb_ref[:, qc]) * (M_DQK ** -0.5)
        k = _conv_silu(xk, tk_ref[:, qc], cw_ref[:, kc], cb_ref[:, kc])
        tq_ref[:, qc] = xq[n - SUBLANES:, :]
        tk_ref[:, qc] = xk[n - SUBLANES:, :]
        qb = q.astype(BF16)
        kb = k.astype(BF16)
        vext = jnp.concatenate([v_ref[:, vc], jnp.ones((n, LANES), BF16)], axis=1)

        gr = gr_ref[hd]
        gc = gc_ref[hd]
        i_row, g_row = gr[0:1, :], gr[1:2, :]
        i_col, g_col = gc[:, 0:1], gc[:, 1:2]
        m = m_ref[hd]

        dmat = jnp.where(causal, g_col - g_row + i_row, NEG)
        inter = g_col + m
        m_row = jnp.maximum(inter, jnp.max(dmat, axis=-1, keepdims=True))
        w_intra = jnp.exp(dmat - m_row)
        a_inter = jnp.exp(inter - m_row)
        s_qk = lax.dot_general(qb, kb, (((1,), (1,)), ((), ())), preferred_element_type=F32) * w_intra
        state = c_ref[hd]
        tot = (jnp.dot(s_qk.astype(BF16), vext, preferred_element_type=F32)
               + a_inter * jnp.dot(qb, state.astype(BF16), preferred_element_type=F32))
        num = tot[:, :M_DV]
        den = tot[:, M_DV:M_DV + 1]
        h = num / jnp.maximum(jnp.abs(den), jnp.exp(-m_row))
        h = h * lax.rsqrt(jnp.mean(h * h, axis=-1, keepdims=True) + EPS)
        o_ref[:, vc] = h.astype(o_ref.dtype)

        g_last = g_col[n - 1:n, :]
        w_k = g_last - g_col + i_col
        m_new = jnp.maximum(g_last + m, jnp.max(w_k, axis=0, keepdims=True))
        decay = jnp.exp(g_last + m - m_new)
        kw = (k * jnp.exp(w_k - m_new)).astype(BF16)
        upd = lax.dot_general(kw, vext, (((0,), (0,)), ((), ())), preferred_element_type=F32)
        c_ref[hd] = decay * state + upd
        m_ref[hd] = m_new


def _mlstm(p, conv_w, conv_b, g_rows, g_cols, chunk):
    s = p.shape[0]
    return pl.pallas_call(
        _mlstm_body,
        out_shape=jax.ShapeDtypeStruct((s, M_V_W), BF16),
        grid=(s // chunk,),
        in_specs=[pl.BlockSpec((chunk, M_QK_W), lambda c: (c, P_MQ // M_QK_W)),
                  pl.BlockSpec((chunk, M_QK_W), lambda c: (c, P_MK // M_QK_W)),
                  pl.BlockSpec((chunk, M_V_W), lambda c: (c, P_MV // M_V_W)),
                  pl.BlockSpec((CONV_K, 2 * M_QK_W), lambda c: (0, 0)),
                  pl.BlockSpec((1, 2 * M_QK_W), lambda c: (0, 0)),
                  pl.BlockSpec((M_HEADS, 2, chunk), lambda c: (0, 0, c)),
                  pl.BlockSpec((M_HEADS, chunk, 2), lambda c: (0, c, 0))],
        out_specs=pl.BlockSpec((chunk, M_V_W), lambda c: (c, 0)),
        scratch_shapes=[pltpu.VMEM((M_HEADS, M_DQK, M_DV + LANES), F32), pltpu.VMEM((M_HEADS, 1, 1), F32),
                        pltpu.VMEM((SUBLANES, M_QK_W), F32), pltpu.VMEM((SUBLANES, M_QK_W), F32)],
        compiler_params=_cparams(("arbitrary",)),
        name="mlstm",
    )(p, p, p, conv_w, conv_b, g_rows, g_cols)


SB_UNDERFLOW = 105.0
SB_HEADS_PER_STEP = 4


def _sb_body(q_ref, k_ref, v_ref, o_ref, *, tile):
    i = pl.program_id(1)
    rows = lax.broadcasted_iota(jnp.int32, (tile, tile), 0)
    cols = lax.broadcasted_iota(jnp.int32, (tile, tile), 1)
    later = (rows > cols).astype(BF16)
    diag_mask = cols < rows

    def head(hh, with_prev):
        lanes = slice(hh * SB_DH, (hh + 1) * SB_DH)
        q = (q_ref[:, lanes].astype(F32) * (SB_DH ** -0.5)).astype(BF16)

        def logits(j, masked):
            kb = k_ref[pl.ds(pl.multiple_of(j * tile, tile), tile), lanes]
            z = lax.dot_general(q, kb, (((1,), (1,)), ((), ())), preferred_element_type=F32)
            sp = jnp.maximum(z, 0.0) + jnp.log(1.0 + jnp.exp(-jnp.abs(z)))
            if masked:
                sp = jnp.where(diag_mask, sp, 0.0)
            within = jnp.dot(sp.astype(BF16), later, preferred_element_type=F32)
            return z, sp, within, jnp.sum(sp, axis=-1, keepdims=True)

        def weighted(j, z, sp, within, rsum, masked):
            vb = v_ref[pl.ds(pl.multiple_of(j * tile, tile), tile), lanes]
            a = jnp.exp((z - sp) - within - rsum)
            if masked:
                a = jnp.where(diag_mask, a, 0.0)
            return jnp.dot(a.astype(BF16), vb, preferred_element_type=F32)

        zd, spd, wd, sd = logits(i, True)
        if not with_prev:
            return weighted(i, zd, spd, wd, jnp.zeros((tile, 1), F32), True), None, None
        zp, spp, wp, sprev = logits(i - 1, False)
        acc = (weighted(i, zd, spd, wd, jnp.zeros((tile, 1), F32), True)
               + weighted(i - 1, zp, spp, wp, sd, False))

        def cond(c):
            return jnp.logical_and(c[0] >= 0, jnp.min(c[1]) <= SB_UNDERFLOW)

        def body(c):
            j, rs, ac = c
            z, sp, w, sj = logits(j, False)
            return j - 1, rs + sj, ac + weighted(j, z, sp, w, rs, False)

        def scan_earlier(rs, ac):
            return lax.while_loop(cond, body, (i - 2, rs, ac))[2]

        return acc, sd + sprev, scan_earlier

    def store(hh, acc):
        o_ref[:, hh * SB_DH:(hh + 1) * SB_DH] = acc.astype(o_ref.dtype)

    @pl.when(i == 0)
    def _():
        for hh in range(SB_HEADS_PER_STEP):
            store(hh, head(hh, False)[0])

    @pl.when(i > 0)
    def _():
        firsts = [head(hh, True) for hh in range(SB_HEADS_PER_STEP)]
        for hh, (acc, rsum, scan_earlier) in enumerate(firsts):
            store(hh, scan_earlier(rsum, acc))


def _stick_breaking(p, tile):
    s = p.shape[0]
    w = SB_HEADS_PER_STEP * SB_DH
    cq, ck, cv = P_SQ // w, P_SK // w, P_SV // w
    return pl.pallas_call(
        functools.partial(_sb_body, tile=tile),
        out_shape=jax.ShapeDtypeStruct((s, SB_W), BF16),
        grid=(SB_HEADS // SB_HEADS_PER_STEP, s // tile),
        in_specs=[pl.BlockSpec((tile, w), lambda h, i: (i, cq + h)),
                  pl.BlockSpec((s, w), lambda h, i: (0, ck + h)),
                  pl.BlockSpec((s, w), lambda h, i: (0, cv + h))],
        out_specs=pl.BlockSpec((tile, w), lambda h, i: (i, h)),
        compiler_params=_cparams(("arbitrary", "arbitrary")),
        name="stick_breaking",
    )(p, p, p)


def _out_proj_body(mo_ref, gm_ref, gs_ref, hm_ref, hs_ref, ng_ref, x_ref, g1_ref, w_ref, o_ref):
    sig = jax.nn.sigmoid
    hm = sig(mo_ref[...].astype(F32)) * (hm_ref[...].astype(F32) * ng_ref[...])
    y = sig(gm_ref[...].astype(F32)) * hm + sig(gs_ref[...].astype(F32)) * hs_ref[...].astype(F32)
    o_ref[...] = x_ref[...] + g1_ref[...] * jnp.dot(y.astype(BF16), w_ref[...], preferred_element_type=F32)


def _out_proj(p, hm, hs, norm_g, x, g1, w_out):
    s, d = x.shape
    tm = min(512, s)
    row = lambda blk: pl.BlockSpec((tm, d), lambda i: (i, blk))
    vec = pl.BlockSpec((1, d), lambda i: (0, 0))
    return pl.pallas_call(
        _out_proj_body,
        out_shape=jax.ShapeDtypeStruct((s, d), F32),
        grid=(s // tm,),
        in_specs=[row(P_MO // d), row(P_GM // d), row(P_GS // d), row(0), row(0), vec, row(0), vec,
                  pl.BlockSpec((d, d), lambda i: (0, 0), pipeline_mode=pl.Buffered(1))],
        out_specs=row(0),
        compiler_params=_cparams(("arbitrary",)),
        name="out_proj",
    )(p, p, p, hm, hs, norm_g, x, g1, w_out)


def _router_body(x_ref, g_ref, sc_ref, sh_ref, rwt_ref, rb_ref, h_ref, eid_ref, wt_ref):
    h = _norm_mod(x_ref[...], g_ref[...], sc_ref[...], sh_ref[...])
    h_ref[...] = h
    logits = lax.dot_general(rwt_ref[...], h, (((1,), (1,)), ((), ())),
                             precision=lax.Precision.HIGHEST, preferred_element_type=F32)
    scores = jax.nn.sigmoid(logits)
    sel = scores + rb_ref[...]
    t = sel.shape[1]
    idx = lax.broadcasted_iota(jnp.int32, (EXPERTS_PER_GROUP, t), 0)
    best = None
    for g in range(N_GROUPS):
        sl = slice(g * EXPERTS_PER_GROUP, (g + 1) * EXPERTS_PER_GROUP)
        s, sc = sel[sl], scores[sl]
        m1 = jnp.max(s, axis=0, keepdims=True)
        i1 = jnp.min(jnp.where(s == m1, idx, EXPERTS_PER_GROUP), axis=0, keepdims=True)
        s2 = jnp.where(idx == i1, -jnp.inf, s)
        m2 = jnp.max(s2, axis=0, keepdims=True)
        i2 = jnp.min(jnp.where(s2 == m2, idx, EXPERTS_PER_GROUP), axis=0, keepdims=True)
        w1 = jnp.sum(jnp.where(idx == i1, sc, 0.0), axis=0, keepdims=True)
        w2 = jnp.sum(jnp.where(idx == i2, sc, 0.0), axis=0, keepdims=True)
        cand = (m1 + m2, i1 + g * EXPERTS_PER_GROUP, i2 + g * EXPERTS_PER_GROUP, w1, w2)
        if best is None:
            best = cand
        else:
            better = cand[0] > best[0]
            best = tuple(jnp.where(better, c, b) for c, b in zip(cand, best))
    _, e1, e2, w1, w2 = best
    wsum = w1 + w2
    eid_ref[...] = jnp.concatenate([e1, e2], axis=0)
    wt_ref[...] = jnp.concatenate([w1 / wsum, w2 / wsum], axis=0)


def _router(x, g, sc, sh, router_wt, router_bias):
    s, d = x.shape
    tm = min(512, s)
    vec = pl.BlockSpec((1, d), lambda i: (0, 0))
    return pl.pallas_call(
        _router_body,
        out_shape=(jax.ShapeDtypeStruct((s, d), F32), jax.ShapeDtypeStruct((TOP_K, s), jnp.int32),
                   jax.ShapeDtypeStruct((TOP_K, s), F32)),
        grid=(s // tm,),
        in_specs=[pl.BlockSpec((tm, d), lambda i: (i, 0)), vec, vec, vec,
                  pl.BlockSpec((N_EXPERTS, d), lambda i: (0, 0)),
                  pl.BlockSpec((N_EXPERTS, 1), lambda i: (0, 0))],
        out_specs=(pl.BlockSpec((tm, d), lambda i: (i, 0)), pl.BlockSpec((TOP_K, tm), lambda i: (0, i)),
                   pl.BlockSpec((TOP_K, tm), lambda i: (0, i))),
        compiler_params=_cparams(("arbitrary",)),
        name="router",
    )(x, g, sc, sh, router_wt, router_bias.reshape(N_EXPERTS, 1))


def _dispatch(eid, bm):
    s = eid.shape[1]
    n_assign = TOP_K * s
    flat_e = eid.reshape(-1)
    onehot = (flat_e[:, None] == jnp.arange(N_EXPERTS, dtype=jnp.int32)[None, :]).astype(jnp.int32)
    csum = jnp.cumsum(onehot, axis=0)
    rank = jnp.sum(csum * onehot, axis=1) - 1
    counts = csum[-1]
    padded = (counts + bm - 1) // bm * bm
    pends = jnp.cumsum(padded)
    dest = (pends - padded)[flat_e] + rank
    nb_max = pl.cdiv(n_assign, bm) + N_EXPERTS
    flat_t = jnp.tile(jnp.arange(s, dtype=jnp.int32), TOP_K)
    slot_tok = (jnp.arange(nb_max * bm, dtype=jnp.int32) % s).at[dest].set(flat_t)
    nb_used = (pends[-1] // bm).astype(jnp.int32).reshape(1)
    block_start = jnp.arange(nb_max, dtype=jnp.int32) * bm
    block_e = jnp.minimum(jnp.sum((pends[None, :] <= block_start[:, None]).astype(jnp.int32), axis=1),
                          N_EXPERTS - 1)
    return dest.astype(jnp.int32), slot_tok, block_e, nb_used, nb_max


def _expert_schedule(block_e, nb_used):
    nb_max = block_e.shape[0]
    idx = jnp.arange(nb_max, dtype=jnp.int32)
    used = idx < nb_used[0]
    prev_e = jnp.concatenate([jnp.full((1,), -1, jnp.int32), block_e[:-1]])
    first = (used & (block_e != prev_e)).astype(jnp.int32)
    wslot = (jnp.cumsum(first) - 1) % 2
    pos = jnp.where(first == 1, idx, nb_max)
    after = jnp.concatenate([lax.cummin(pos[::-1])[::-1][1:], jnp.full((1,), nb_max, jnp.int32)])
    nxt = jnp.where(after < nb_max, block_e[jnp.minimum(after, nb_max - 1)], -1)
    return first, wslot.astype(jnp.int32), nxt.astype(jnp.int32)


def _expert_body(be_ref, first_ref, ws_ref, nxt_ref, st_ref, nu_ref, h_hbm, wg_hbm, wu_hbm, wd_hbm, y_ref,
                 xa_ref, xb_ref, wg_buf, wu_buf, wd_buf, gsem, wsem, *, bm, layer):
    b = pl.program_id(0)
    nb = nu_ref[0]
    n_chunks = D_FF_EXPERT // MOE_FF_CHUNK

    def row_copy(tok, r, buf, si):
        return pltpu.make_async_copy(h_hbm.at[pl.ds(tok, 1), :], buf.at[pl.ds(r, 1), :], gsem.at[si])

    def wait_gather(buf, si):
        pltpu.make_async_copy(h_hbm.at[pl.ds(0, bm), :], buf, gsem.at[si]).wait()

    def weight_copies(e, slot):
        return (pltpu.make_async_copy(wg_hbm.at[layer, e], wg_buf.at[slot], wsem.at[slot, 0]),
                pltpu.make_async_copy(wu_hbm.at[layer, e], wu_buf.at[slot], wsem.at[slot, 1]),
                pltpu.make_async_copy(wd_hbm.at[layer, e], wd_buf.at[slot], wsem.at[slot, 2]))

    @pl.when(b == 0)
    def _():
        for cp in weight_copies(be_ref[0], 0):
            cp.start()

        def go(r, c):
            row_copy(st_ref[r], r, xa_ref, 0).start()
            return c
        lax.fori_loop(0, bm, go, 0, unroll=MOE_GATHER_UNROLL)

    def step(cur, cur_si, nxt_buf, nxt_si):
        slot = ws_ref[b]

        @pl.when(first_ref[b] == 1)
        def _():
            for cp in weight_copies(be_ref[b], slot):
                cp.wait()

            @pl.when(nxt_ref[b] >= 0)
            def _():
                for cp in weight_copies(nxt_ref[b], 1 - slot):
                    cp.start()

        wait_gather(cur, cur_si)
        base = jnp.minimum(b + 1, nb - 1) * bm
        x = cur[...].astype(BF16)
        y = None
        for c in range(n_chunks):
            for r in range(c * bm // n_chunks, (c + 1) * bm // n_chunks):
                row_copy(st_ref[base + r], r, nxt_buf, nxt_si).start(priority=r % 2)
            cs = slice(c * MOE_FF_CHUNK, (c + 1) * MOE_FF_CHUNK)
            gate = jnp.dot(x, wg_buf[slot, :, cs].astype(BF16), preferred_element_type=F32)
            up = jnp.dot(x, wu_buf[slot, :, cs].astype(BF16), preferred_element_type=F32)
            mid = (gate * jax.nn.sigmoid(gate) * up).astype(BF16)
            part = jnp.dot(mid, wd_buf[slot, cs, :].astype(BF16), preferred_element_type=F32)
            y = part if y is None else y + part
        y_ref[...] = y

        @pl.when(b == nb - 1)
        def _():
            wait_gather(nxt_buf, nxt_si)

    @pl.when((b < nb) & (b % 2 == 0))
    def _():
        step(xa_ref, 0, xb_ref, 1)

    @pl.when((b < nb) & (b % 2 == 1))
    def _():
        step(xb_ref, 1, xa_ref, 0)

    @pl.when(b >= nb)
    def _():
        y_ref[...] = jnp.zeros_like(y_ref)


def _experts(h, slot_tok, block_e, nb_used, nb_max, layer, w_gate, w_up, w_down, bm):
    d = h.shape[1]
    ff = D_FF_EXPERT
    first, wslot, nxt = _expert_schedule(block_e, nb_used)
    hbm = pl.BlockSpec(memory_space=pl.ANY)
    grid_spec = pltpu.PrefetchScalarGridSpec(
        num_scalar_prefetch=6,
        grid=(nb_max,),
        in_specs=[hbm, hbm, hbm, hbm],
        out_specs=pl.BlockSpec((bm, d), lambda b, *_: (b, 0)),
        scratch_shapes=[pltpu.VMEM((bm, d), F32), pltpu.VMEM((bm, d), F32),
                        pltpu.VMEM((2, d, ff), F32), pltpu.VMEM((2, d, ff), F32), pltpu.VMEM((2, ff, d), F32),
                        pltpu.SemaphoreType.DMA((2,)), pltpu.SemaphoreType.DMA((2, 3))],
    )
    return pl.pallas_call(
        functools.partial(_expert_body, bm=bm, layer=layer),
        out_shape=jax.ShapeDtypeStruct((nb_max * bm, d), F32),
        grid_spec=grid_spec,
        compiler_params=pltpu.CompilerParams(dimension_semantics=("arbitrary",),
                                             vmem_limit_bytes=MOE_VMEM_LIMIT),
        name="experts",
    )(block_e, first, wslot, nxt, slot_tok, nb_used, h, w_gate, w_up, w_down)


def _combine_body(pos_ref, y_hbm, w_ref, x_ref, g2_ref, ng_ref, *rest, tc, s, final):
    if final:
        o_ref, yb_ref, sem = rest
    else:
        nsc_ref, nsh_ref, o_ref, h_ref, yb_ref, sem = rest
    i = pl.program_id(0)
    n = pl.num_programs(0)

    def start_gather(blk, slot):
        def go(r, c):
            for k in range(TOP_K):
                src = pos_ref[k * s + blk * tc + r]
                pltpu.make_async_copy(y_hbm.at[pl.ds(src, 1), :], yb_ref.at[slot, k, pl.ds(r, 1), :],
                                      sem.at[slot]).start()
            return c
        lax.fori_loop(0, tc, go, 0, unroll=MOE_GATHER_UNROLL // TOP_K)

    def wait_gather(slot):
        for k in range(TOP_K):
            pltpu.make_async_copy(y_hbm.at[pl.ds(0, tc), :], yb_ref.at[slot, k], sem.at[slot]).wait()

    slot = i % 2

    @pl.when(i == 0)
    def _():
        start_gather(0, 0)

    @pl.when(i + 1 < n)
    def _():
        start_gather(i + 1, 1 - slot)

    wait_gather(slot)
    w = w_ref[...]
    moe = w[:, 0:1] * yb_ref[slot, 0] + w[:, 1:2] * yb_ref[slot, 1]
    xn = x_ref[...] + g2_ref[...] * moe
    if final:
        o_ref[...] = (xn * lax.rsqrt(jnp.mean(xn * xn, axis=-1, keepdims=True) + EPS)) * ng_ref[...]
    else:
        o_ref[...] = xn
        h_ref[...] = _norm_mod(xn, ng_ref[...], nsc_ref[...], nsh_ref[...]).astype(h_ref.dtype)


def _combine(ys, pos, wts, x, g2, tc, norm_g, norm_sc=None, norm_sh=None):
    s, d = x.shape
    final = norm_sc is None
    row = pl.BlockSpec((tc, d), lambda i, pos: (i, 0))
    vec = pl.BlockSpec((1, d), lambda i, pos: (0, 0))
    norm_args = (norm_g,) if final else (norm_g, norm_sc, norm_sh)
    grid_spec = pltpu.PrefetchScalarGridSpec(
        num_scalar_prefetch=1,
        grid=(s // tc,),
        in_specs=[pl.BlockSpec(memory_space=pl.ANY), pl.BlockSpec((tc, TOP_K), lambda i, pos: (i, 0)), row, vec]
        + [vec] * len(norm_args),
        out_specs=row if final else (row, row),
        scratch_shapes=[pltpu.VMEM((2, TOP_K, tc, d), F32), pltpu.SemaphoreType.DMA((2,))],
    )
    out_f32 = jax.ShapeDtypeStruct((s, d), F32)
    return pl.pallas_call(
        functools.partial(_combine_body, tc=tc, s=s, final=final),
        out_shape=out_f32 if final else (out_f32, jax.ShapeDtypeStruct((s, d), BF16)),
        grid_spec=grid_spec,
        compiler_params=_cparams(("arbitrary",)),
        name="moe_combine",
    )(pos, ys, wts, x, g2, *norm_args)


def kernel(x, c, ada_w, ada_b, norm1_g, w_in, m_conv_w, m_conv_b, m_igate_b, m_fgate_b, m_norm_g, w_out,
           norm2_g, router_w, router_bias, w_gate, w_up, w_down, final_g):
    batch, s, d = x.shape
    assert batch == 1 and d == D_MODEL
    chunk = min(MLSTM_CHUNK, s)
    sb_tile = min(SB_TILE, s)
    tc = min(COMBINE_TC, s)
    xs = x.reshape(s, d)
    mod = _ada(c, ada_w, ada_b)
    router_wt = router_w.T
    w_in_t = jnp.swapaxes(w_in, 1, 2)
    mods = [[mod[l, :, i * d:(i + 1) * d] for i in range(6)] for l in range(DEPTH)]
    h1 = _norm1(xs, norm1_g[0].reshape(1, d), mods[0][1], mods[0][0])
    for l in range(DEPTH):
        sh1, sc1, g1, sh2, sc2, g2 = mods[l]
        p = _in_proj(h1, w_in_t, l)
        gate_bias = jnp.concatenate([m_igate_b[l], m_fgate_b[l]]).reshape(GATE_COLS, 1)
        gproc = _gates(h1, w_in_t, gate_bias, l, chunk)
        g_rows = gproc.reshape(2, M_HEADS, s).transpose(1, 0, 2)
        g_cols = g_rows.transpose(0, 2, 1)
        hm = _mlstm(p, m_conv_w[l], m_conv_b[l].reshape(1, -1), g_rows, g_cols, chunk)
        hs = _stick_breaking(p, sb_tile)
        xs = _out_proj(p, hm, hs, m_norm_g[l].reshape(1, d), xs, g1, w_out[l].astype(BF16))
        h2, eid, wts = _router(xs, norm2_g[l].reshape(1, d), sc2, sh2, router_wt, router_bias)
        dest, slot_tok, block_e, nb_used, nb_max = _dispatch(eid, MOE_BM)
        ys = _experts(h2, slot_tok, block_e, nb_used, nb_max, l, w_gate, w_up, w_down, MOE_BM)
        if l + 1 < DEPTH:
            xs, h1 = _combine(ys, dest, wts.T, xs, g2, tc, norm1_g[l + 1].reshape(1, d),
                              mods[l + 1][1], mods[l + 1][0])
        else:
            out = _combine(ys, dest, wts.T, xs, g2, tc, final_g.reshape(1, d))
    return out.reshape(batch, s, d)
```

```python
import functools

import jax
import jax.numpy as jnp
from jax import lax
from jax.experimental import pallas as pl
from jax.experimental.pallas import tpu as pltpu

F32 = jnp.float32
BF16 = jnp.bfloat16

D_MODEL = 2048
DEPTH = 2
M_HEADS = 4
M_DV = D_MODEL // M_HEADS
M_DQK = M_DV // 2
CONV_K = 4
SB_HEADS = 16
SB_DH = D_MODEL // SB_HEADS
N_EXPERTS = 32
N_GROUPS = 4
EXPERTS_PER_GROUP = N_EXPERTS // N_GROUPS
TOP_K = 2
D_FF_EXPERT = 768
EPS = 1e-6
NEG = -1e30

M_QK_W = M_HEADS * M_DQK
M_V_W = M_HEADS * M_DV
SB_W = SB_HEADS * SB_DH
GATE_COL0 = 2 * M_QK_W + 2 * M_V_W
GATE_COLS = 2 * M_HEADS
P_MQ, P_MK, P_MV, P_MO = 0, M_QK_W, 2 * M_QK_W, 2 * M_QK_W + M_V_W
P_SQ = GATE_COL0
P_SK, P_SV, P_GM, P_GS = P_SQ + SB_W, P_SQ + 2 * SB_W, P_SQ + 3 * SB_W, P_SQ + 4 * SB_W
P_W = P_SQ + 5 * SB_W

LANES = 128
SUBLANES = 8
VMEM_LIMIT = 56 * 1024 * 1024

MLSTM_CHUNK = 256
SB_TILE = 256
MOE_BM = 512
MOE_FF_CHUNK = 256
MOE_GATHER_UNROLL = 8
MOE_DISPATCH_ROWS = 512
MOE_VMEM_LIMIT = 60 * 1024 * 1024
COMBINE_TC = 256


def _cparams(sem):
    return pltpu.CompilerParams(dimension_semantics=sem, vmem_limit_bytes=VMEM_LIMIT)


def _ada_body(c_ref, w_ref, b_ref, o_ref):
    c = c_ref[...]
    cs = c * jax.nn.sigmoid(c)
    for j in range(w_ref.shape[2] // LANES):
        sl = slice(j * LANES, (j + 1) * LANES)
        o_ref[0, :, sl] = jnp.sum(w_ref[0, :, sl] * cs, axis=0, keepdims=True) + b_ref[0, :, sl]


def _ada(c, ada_w, ada_b):
    depth, d, n = ada_w.shape
    tn = 1024
    cb = jnp.broadcast_to(c.reshape(d, 1), (d, LANES))
    return pl.pallas_call(
        _ada_body,
        out_shape=jax.ShapeDtypeStruct((depth, 1, n), F32),
        grid=(depth, n // tn),
        in_specs=[pl.BlockSpec((d, LANES), lambda l, j: (0, 0)),
                  pl.BlockSpec((1, d, tn), lambda l, j: (l, 0, j)),
                  pl.BlockSpec((1, 1, tn), lambda l, j: (l, 0, j))],
        out_specs=pl.BlockSpec((1, 1, tn), lambda l, j: (l, 0, j)),
        compiler_params=_cparams(("arbitrary", "arbitrary")),
        name="ada_mod",
    )(cb, ada_w, ada_b.reshape(depth, 1, n))


def _norm_mod(x, g, sc, sh):
    y = x * lax.rsqrt(jnp.mean(x * x, axis=-1, keepdims=True) + EPS)
    return (y * g) * (1.0 + sc) + sh


def _gate_rows(h, w, b):
    v = lax.dot_general(w.astype(BF16), h, (((1,), (1,)), ((), ())), preferred_element_type=F32) + b
    n = v.shape[1]
    logf = jnp.minimum(v, 0.0) - jnp.log(1.0 + jnp.exp(-jnp.abs(v)))
    upper = (lax.broadcasted_iota(jnp.int32, (n, n), 0)
             <= lax.broadcasted_iota(jnp.int32, (n, n), 1)).astype(F32)
    csum = jnp.dot(logf, upper, precision=lax.Precision.HIGHEST, preferred_element_type=F32)
    row = lax.broadcasted_iota(jnp.int32, v.shape, 0)
    return jnp.where(row < M_HEADS, v, csum)


def _gate_w_spec(layer):
    return pl.BlockSpec((1, GATE_COLS, D_MODEL), lambda *_: (layer, GATE_COL0 // GATE_COLS, 0))


def _norm1_body(x_ref, g_ref, sc_ref, sh_ref, gw_ref, gb_ref, o_ref, gate_ref):
    h = _norm_mod(x_ref[...], g_ref[...], sc_ref[...], sh_ref[...]).astype(o_ref.dtype)
    o_ref[...] = h
    gate_ref[...] = _gate_rows(h, gw_ref[0], gb_ref[...])


def _norm1(x, g, sc, sh, w_in_t, gate_bias, layer, chunk):
    s, d = x.shape
    vec = pl.BlockSpec((1, d), lambda i: (0, 0))
    return pl.pallas_call(
        _norm1_body,
        out_shape=(jax.ShapeDtypeStruct((s, d), BF16), jax.ShapeDtypeStruct((GATE_COLS, s), F32)),
        grid=(s // chunk,),
        in_specs=[pl.BlockSpec((chunk, d), lambda i: (i, 0)), vec, vec, vec, _gate_w_spec(layer),
                  pl.BlockSpec((GATE_COLS, 1), lambda i: (0, 0))],
        out_specs=(pl.BlockSpec((chunk, d), lambda i: (i, 0)), pl.BlockSpec((GATE_COLS, chunk), lambda i: (0, i))),
        compiler_params=_cparams(("arbitrary",)),
        name="norm1",
    )(x, g, sc, sh, w_in_t, gate_bias)


IN_PROJ_TN = 1024


def _in_proj_body(h_ref, wt_ref, p_ref, w_ref):
    @pl.when(pl.program_id(1) == 0)
    def _():
        w_ref[...] = wt_ref[0].astype(BF16)

    p_ref[...] = lax.dot_general(h_ref[...], w_ref[...], (((1,), (1,)), ((), ())),
                                 preferred_element_type=F32).astype(p_ref.dtype)


def _in_proj(h, w_in_t, layer):
    s, d = h.shape
    tm, tn = min(1024, s), IN_PROJ_TN

    def w_rows(j, i):
        tiles = j * (tn // GATE_COLS) + jnp.where(j >= GATE_COL0 // tn, 1, 0)
        return (layer, pl.multiple_of(tiles * GATE_COLS, GATE_COLS), 0)

    return pl.pallas_call(
        _in_proj_body,
        out_shape=jax.ShapeDtypeStruct((s, P_W), BF16),
        grid=(P_W // tn, s // tm),
        in_specs=[pl.BlockSpec((tm, d), lambda j, i: (i, 0)),
                  pl.BlockSpec((pl.Element(1), pl.Element(tn), pl.Element(d)), w_rows)],
        out_specs=pl.BlockSpec((tm, tn), lambda j, i: (i, j)),
        scratch_shapes=[pltpu.VMEM((tn, d), BF16)],
        compiler_params=_cparams(("arbitrary", "arbitrary")),
        name="in_proj",
    )(h, w_in_t)


def _conv_silu(x, tail, w, b):
    row8 = lax.broadcasted_iota(jnp.int32, tail.shape, 0)
    y = b + w[CONV_K - 1:CONV_K, :] * x
    for d in range(1, CONV_K):
        rolled = pltpu.roll(x, d, 0)
        head = jnp.where(row8 < d, pltpu.roll(tail, d, 0), rolled[:SUBLANES])
        xd = jnp.concatenate([head, rolled[SUBLANES:]], axis=0)
        y = y + w[CONV_K - 1 - d:CONV_K - d, :] * xd
    return y * jax.nn.sigmoid(y)


def _mlstm_body(q_ref, k_ref, v_ref, cw_ref, cb_ref, gr_ref, gc_ref, o_ref, c_ref, m_ref, tq_ref, tk_ref):
    @pl.when(pl.program_id(0) == 0)
    def _():
        c_ref[...] = jnp.zeros_like(c_ref)
        m_ref[...] = jnp.zeros_like(m_ref)
        tq_ref[...] = jnp.zeros_like(tq_ref)
        tk_ref[...] = jnp.zeros_like(tk_ref)

    n = q_ref.shape[0]
    causal = (lax.broadcasted_iota(jnp.int32, (n, n), 1) <= lax.broadcasted_iota(jnp.int32, (n, n), 0))
    for hd in range(M_HEADS):
        qc = slice(hd * M_DQK, (hd + 1) * M_DQK)
        kc = slice(M_QK_W + hd * M_DQK, M_QK_W + (hd + 1) * M_DQK)
        vc = slice(hd * M_DV, (hd + 1) * M_DV)
        xq = q_ref[:, qc].astype(F32)
        xk = k_ref[:, qc].astype(F32)
        q = _conv_silu(xq, tq_ref[:, qc], cw_ref[:, qc], cb_ref[:, qc]) * (M_DQK ** -0.5)
        k = _conv_silu(xk, tk_ref[:, qc], cw_ref[:, kc], cb_ref[:, kc])
        tq_ref[:, qc] = xq[n - SUBLANES:, :]
        tk_ref[:, qc] = xk[n - SUBLANES:, :]
        qb = q.astype(BF16)
        kb = k.astype(BF16)
        vext = jnp.concatenate([v_ref[:, vc], jnp.ones((n, LANES), BF16)], axis=1)

        gr = gr_ref[hd]
        gc = gc_ref[hd]
        i_row, g_row = gr[0:1, :], gr[1:2, :]
        i_col, g_col = gc[:, 0:1], gc[:, 1:2]
        m = m_ref[hd]

        dmat = jnp.where(causal, g_col - g_row + i_row, NEG)
        inter = g_col + m
        m_row = jnp.maximum(inter, jnp.max(dmat, axis=-1, keepdims=True))
        w_intra = jnp.exp(dmat - m_row)
        a_inter = jnp.exp(inter - m_row)
        s_qk = lax.dot_general(qb, kb, (((1,), (1,)), ((), ())), preferred_element_type=F32) * w_intra
        state = c_ref[hd]
        tot = (jnp.dot(s_qk.astype(BF16), vext, preferred_element_type=F32)
               + a_inter * jnp.dot(qb, state.astype(BF16), preferred_element_type=F32))
        num = tot[:, :M_DV]
        den = tot[:, M_DV:M_DV + 1]
        h = num / jnp.maximum(jnp.abs(den), jnp.exp(-m_row))
        h = h * lax.rsqrt(jnp.mean(h * h, axis=-1, keepdims=True) + EPS)
        o_ref[:, vc] = h.astype(o_ref.dtype)

        g_last = g_col[n - 1:n, :]
        w_k = g_last - g_col + i_col
        m_new = jnp.maximum(g_last + m, jnp.max(w_k, axis=0, keepdims=True))
        decay = jnp.exp(g_last + m - m_new)
        kw = (k * jnp.exp(w_k - m_new)).astype(BF16)
        upd = lax.dot_general(kw, vext, (((0,), (0,)), ((), ())), preferred_element_type=F32)
        c_ref[hd] = decay * state + upd
        m_ref[hd] = m_new


def _mlstm(p, conv_w, conv_b, g_rows, g_cols, chunk):
    s = p.shape[0]
    return pl.pallas_call(
        _mlstm_body,
        out_shape=jax.ShapeDtypeStruct((s, M_V_W), BF16),
        grid=(s // chunk,),
        in_specs=[pl.BlockSpec((chunk, M_QK_W), lambda c: (c, P_MQ // M_QK_W)),
                  pl.BlockSpec((chunk, M_QK_W), lambda c: (c, P_MK // M_QK_W)),
                  pl.BlockSpec((chunk, M_V_W), lambda c: (c, P_MV // M_V_W)),
                  pl.BlockSpec((CONV_K, 2 * M_QK_W), lambda c: (0, 0)),
                  pl.BlockSpec((1, 2 * M_QK_W), lambda c: (0, 0)),
                  pl.BlockSpec((M_HEADS, 2, chunk), lambda c: (0, 0, c)),
                  pl.BlockSpec((M_HEADS, chunk, 2), lambda c: (0, c, 0))],
        out_specs=pl.BlockSpec((chunk, M_V_W), lambda c: (c, 0)),
        scratch_shapes=[pltpu.VMEM((M_HEADS, M_DQK, M_DV + LANES), F32), pltpu.VMEM((M_HEADS, 1, 1), F32),
                        pltpu.VMEM((SUBLANES, M_QK_W), F32), pltpu.VMEM((SUBLANES, M_QK_W), F32)],
        compiler_params=_cparams(("arbitrary",)),
        name="mlstm",
    )(p, p, p, conv_w, conv_b, g_rows, g_cols)


SB_UNDERFLOW = 105.0
SB_HEADS_PER_STEP = 4


def _sb_body(q_ref, k_ref, v_ref, o_ref, *, tile):
    i = pl.program_id(1)
    rows = lax.broadcasted_iota(jnp.int32, (tile, tile), 0)
    cols = lax.broadcasted_iota(jnp.int32, (tile, tile), 1)
    later = (rows > cols).astype(BF16)
    diag_mask = cols < rows

    def head(hh, with_prev):
        lanes = slice(hh * SB_DH, (hh + 1) * SB_DH)
        q = (q_ref[:, lanes].astype(F32) * (SB_DH ** -0.5)).astype(BF16)

        def logits(j, masked):
            kb = k_ref[pl.ds(pl.multiple_of(j * tile, tile), tile), lanes]
            z = lax.dot_general(q, kb, (((1,), (1,)), ((), ())), preferred_element_type=F32)
            sp = jnp.maximum(z, 0.0) + jnp.log(1.0 + jnp.exp(-jnp.abs(z)))
            if masked:
                sp = jnp.where(diag_mask, sp, 0.0)
            within = jnp.dot(sp.astype(BF16), later, preferred_element_type=F32)
            return z, sp, within, jnp.sum(sp, axis=-1, keepdims=True)

        def weighted(j, z, sp, within, rsum, masked):
            vb = v_ref[pl.ds(pl.multiple_of(j * tile, tile), tile), lanes]
            a = jnp.exp((z - sp) - within - rsum)
            if masked:
                a = jnp.where(diag_mask, a, 0.0)
            return jnp.dot(a.astype(BF16), vb, preferred_element_type=F32)

        zd, spd, wd, sd = logits(i, True)
        if not with_prev:
            return weighted(i, zd, spd, wd, jnp.zeros((tile, 1), F32), True), None, None
        zp, spp, wp, sprev = logits(i - 1, False)
        acc = (weighted(i, zd, spd, wd, jnp.zeros((tile, 1), F32), True)
               + weighted(i - 1, zp, spp, wp, sd, False))

        def cond(c):
            return jnp.logical_and(c[0] >= 0, jnp.min(c[1]) <= SB_UNDERFLOW)

        def body(c):
            j, rs, ac = c
            z, sp, w, sj = logits(j, False)
            return j - 1, rs + sj, ac + weighted(j, z, sp, w, rs, False)

        def scan_earlier(rs, ac):
            return lax.while_loop(cond, body, (i - 2, rs, ac))[2]

        return acc, sd + sprev, scan_earlier

    def store(hh, acc):
        o_ref[:, hh * SB_DH:(hh + 1) * SB_DH] = acc.astype(o_ref.dtype)

    @pl.when(i == 0)
    def _():
        for hh in range(SB_HEADS_PER_STEP):
            store(hh, head(hh, False)[0])

    @pl.when(i > 0)
    def _():
        firsts = [head(hh, True) for hh in range(SB_HEADS_PER_STEP)]
        for hh, (acc, rsum, scan_earlier) in enumerate(firsts):
            store(hh, scan_earlier(rsum, acc))


def _stick_breaking(p, tile):
    s = p.shape[0]
    w = SB_HEADS_PER_STEP * SB_DH
    cq, ck, cv = P_SQ // w, P_SK // w, P_SV // w
    return pl.pallas_call(
        functools.partial(_sb_body, tile=tile),
        out_shape=jax.ShapeDtypeStruct((s, SB_W), BF16),
        grid=(SB_HEADS // SB_HEADS_PER_STEP, s // tile),
        in_specs=[pl.BlockSpec((tile, w), lambda h, i: (i, cq + h)),
                  pl.BlockSpec((s, w), lambda h, i: (0, ck + h)),
                  pl.BlockSpec((s, w), lambda h, i: (0, cv + h))],
        out_specs=pl.BlockSpec((tile, w), lambda h, i: (i, h)),
        compiler_params=_cparams(("arbitrary", "arbitrary")),
        name="stick_breaking",
    )(p, p, p)


def _out_proj_body(mo_ref, gm_ref, gs_ref, hm_ref, hs_ref, ng_ref, x_ref, g1_ref, w_ref, o_ref):
    sig = jax.nn.sigmoid
    hm = sig(mo_ref[...].astype(F32)) * (hm_ref[...].astype(F32) * ng_ref[...])
    y = sig(gm_ref[...].astype(F32)) * hm + sig(gs_ref[...].astype(F32)) * hs_ref[...].astype(F32)
    o_ref[...] = x_ref[...] + g1_ref[...] * jnp.dot(y.astype(BF16), w_ref[...], preferred_element_type=F32)


def _out_proj(p, hm, hs, norm_g, x, g1, w_out):
    s, d = x.shape
    tm = min(512, s)
    row = lambda blk: pl.BlockSpec((tm, d), lambda i: (i, blk))
    vec = pl.BlockSpec((1, d), lambda i: (0, 0))
    return pl.pallas_call(
        _out_proj_body,
        out_shape=jax.ShapeDtypeStruct((s, d), F32),
        grid=(s // tm,),
        in_specs=[row(P_MO // d), row(P_GM // d), row(P_GS // d), row(0), row(0), vec, row(0), vec,
                  pl.BlockSpec((d, d), lambda i: (0, 0), pipeline_mode=pl.Buffered(1))],
        out_specs=row(0),
        compiler_params=_cparams(("arbitrary",)),
        name="out_proj",
    )(p, p, p, hm, hs, norm_g, x, g1, w_out)


def _router_body(x_ref, g_ref, sc_ref, sh_ref, rwt_ref, rb_ref, h_ref, eid_ref, wt_ref):
    h = _norm_mod(x_ref[...], g_ref[...], sc_ref[...], sh_ref[...])
    h_ref[...] = h
    logits = lax.dot_general(rwt_ref[...], h, (((1,), (1,)), ((), ())),
                             precision=lax.Precision.HIGHEST, preferred_element_type=F32)
    scores = jax.nn.sigmoid(logits)
    sel = scores + rb_ref[...]
    t = sel.shape[1]
    idx = lax.broadcasted_iota(jnp.int32, (EXPERTS_PER_GROUP, t), 0)
    best = None
    for g in range(N_GROUPS):
        sl = slice(g * EXPERTS_PER_GROUP, (g + 1) * EXPERTS_PER_GROUP)
        s, sc = sel[sl], scores[sl]
        m1 = jnp.max(s, axis=0, keepdims=True)
        i1 = jnp.min(jnp.where(s == m1, idx, EXPERTS_PER_GROUP), axis=0, keepdims=True)
        s2 = jnp.where(idx == i1, -jnp.inf, s)
        m2 = jnp.max(s2, axis=0, keepdims=True)
        i2 = jnp.min(jnp.where(s2 == m2, idx, EXPERTS_PER_GROUP), axis=0, keepdims=True)
        w1 = jnp.sum(jnp.where(idx == i1, sc, 0.0), axis=0, keepdims=True)
        w2 = jnp.sum(jnp.where(idx == i2, sc, 0.0), axis=0, keepdims=True)
        cand = (m1 + m2, i1 + g * EXPERTS_PER_GROUP, i2 + g * EXPERTS_PER_GROUP, w1, w2)
        if best is None:
            best = cand
        else:
            better = cand[0] > best[0]
            best = tuple(jnp.where(better, c, b) for c, b in zip(cand, best))
    _, e1, e2, w1, w2 = best
    wsum = w1 + w2
    eid_ref[...] = jnp.concatenate([e1, e2], axis=0)
    wt_ref[...] = jnp.concatenate([w1 / wsum, w2 / wsum], axis=0)


def _router(x, g, sc, sh, router_wt, router_bias):
    s, d = x.shape
    tm = min(512, s)
    vec = pl.BlockSpec((1, d), lambda i: (0, 0))
    return pl.pallas_call(
        _router_body,
        out_shape=(jax.ShapeDtypeStruct((s, d), F32), jax.ShapeDtypeStruct((TOP_K, s), jnp.int32),
                   jax.ShapeDtypeStruct((TOP_K, s), F32)),
        grid=(s // tm,),
        in_specs=[pl.BlockSpec((tm, d), lambda i: (i, 0)), vec, vec, vec,
                  pl.BlockSpec((N_EXPERTS, d), lambda i: (0, 0)),
                  pl.BlockSpec((N_EXPERTS, 1), lambda i: (0, 0))],
        out_specs=(pl.BlockSpec((tm, d), lambda i: (i, 0)), pl.BlockSpec((TOP_K, tm), lambda i: (0, i)),
                   pl.BlockSpec((TOP_K, tm), lambda i: (0, i))),
        compiler_params=_cparams(("arbitrary",)),
        name="router",
    )(x, g, sc, sh, router_wt, router_bias.reshape(N_EXPERTS, 1))


def _dispatch(eid, bm):
    s = eid.shape[1]
    n_assign = TOP_K * s
    flat_e = eid.reshape(-1)
    onehot = (flat_e[:, None] == jnp.arange(N_EXPERTS, dtype=jnp.int32)[None, :]).astype(jnp.int32)
    csum = jnp.cumsum(onehot, axis=0)
    rank = jnp.sum(csum * onehot, axis=1) - 1
    counts = csum[-1]
    padded = (counts + bm - 1) // bm * bm
    pends = jnp.cumsum(padded)
    dest = (pends - padded)[flat_e] + rank
    nb_max = pl.cdiv(n_assign, bm) + N_EXPERTS
    flat_t = jnp.tile(jnp.arange(s, dtype=jnp.int32), TOP_K)
    slot_tok = (jnp.arange(nb_max * bm, dtype=jnp.int32) % s).at[dest].set(flat_t)
    nb_used = (pends[-1] // bm).astype(jnp.int32).reshape(1)
    block_start = jnp.arange(nb_max, dtype=jnp.int32) * bm
    block_e = jnp.minimum(jnp.sum((pends[None, :] <= block_start[:, None]).astype(jnp.int32), axis=1),
                          N_EXPERTS - 1)
    return dest.astype(jnp.int32), slot_tok, block_e, nb_used, nb_max


def _expert_schedule(block_e, nb_used):
    nb_max = block_e.shape[0]
    idx = jnp.arange(nb_max, dtype=jnp.int32)
    used = idx < nb_used[0]
    prev_e = jnp.concatenate([jnp.full((1,), -1, jnp.int32), block_e[:-1]])
    first = (used & (block_e != prev_e)).astype(jnp.int32)
    pos = jnp.where(first == 1, idx, nb_max)
    after = jnp.concatenate([lax.cummin(pos[::-1])[::-1][1:], jnp.full((1,), nb_max, jnp.int32)])
    nxt = jnp.where(after < nb_max, block_e[jnp.minimum(after, nb_max - 1)], -1)
    return first, nxt.astype(jnp.int32)


def _expert_body(be_ref, first_ref, nxt_ref, st_ref, nu_ref, h_hbm, wg_hbm, wu_hbm, wd_hbm, y_ref,
                 xa_ref, xb_ref, wg_st, wu_st, wd_st, wg_bf, wu_bf, wd_bf, gsem, wsem, *, bm, layer):
    b = pl.program_id(0)
    nb = nu_ref[0]
    n_chunks = D_FF_EXPERT // MOE_FF_CHUNK

    def row_copy(tok, r, buf, si):
        return pltpu.make_async_copy(h_hbm.at[pl.ds(tok, 1), :], buf.at[pl.ds(r, 1), :], gsem.at[si])

    def wait_gather(buf, si):
        pltpu.make_async_copy(h_hbm.at[pl.ds(0, bm), :], buf, gsem.at[si]).wait()

    def weight_copies(e):
        return (pltpu.make_async_copy(wg_hbm.at[layer, e], wg_st, wsem.at[0]),
                pltpu.make_async_copy(wu_hbm.at[layer, e], wu_st, wsem.at[1]),
                pltpu.make_async_copy(wd_hbm.at[layer, e], wd_st, wsem.at[2]))

    @pl.when(b == 0)
    def _():
        for cp in weight_copies(be_ref[0]):
            cp.start()

        def go(r, c):
            row_copy(st_ref[r], r, xa_ref, 0).start()
            return c
        lax.fori_loop(0, bm, go, 0, unroll=MOE_GATHER_UNROLL)

    @pl.when((b < nb) & (first_ref[b] == 1))
    def _():
        for cp in weight_copies(be_ref[b]):
            cp.wait()
        for c in range(n_chunks):
            cs = slice(c * MOE_FF_CHUNK, (c + 1) * MOE_FF_CHUNK)
            wg_bf[:, cs] = wg_st[:, cs].astype(BF16)
            wu_bf[:, cs] = wu_st[:, cs].astype(BF16)
            wd_bf[cs, :] = wd_st[cs, :].astype(BF16)

        @pl.when(nxt_ref[b] >= 0)
        def _():
            for cp in weight_copies(nxt_ref[b]):
                cp.start()

    def step(cur, cur_si, nxt_buf, nxt_si):
        wait_gather(cur, cur_si)
        base = jnp.minimum(b + 1, nb - 1) * bm
        x = cur[...].astype(BF16)
        y = None
        for c in range(n_chunks):
            for r in range(c * bm // n_chunks, (c + 1) * bm // n_chunks):
                row_copy(st_ref[base + r], r, nxt_buf, nxt_si).start(priority=r % 2)
            cs = slice(c * MOE_FF_CHUNK, (c + 1) * MOE_FF_CHUNK)
            gate = jnp.dot(x, wg_bf[:, cs], preferred_element_type=F32)
            up = jnp.dot(x, wu_bf[:, cs], preferred_element_type=F32)
            mid = (gate * jax.nn.sigmoid(gate) * up).astype(BF16)
            part = jnp.dot(mid, wd_bf[cs, :], preferred_element_type=F32)
            y = part if y is None else y + part
        y_ref[...] = y

        @pl.when(b == nb - 1)
        def _():
            wait_gather(nxt_buf, nxt_si)

    @pl.when((b < nb) & (b % 2 == 0))
    def _():
        step(xa_ref, 0, xb_ref, 1)

    @pl.when((b < nb) & (b % 2 == 1))
    def _():
        step(xb_ref, 1, xa_ref, 0)

    @pl.when(b >= nb)
    def _():
        y_ref[...] = jnp.zeros_like(y_ref)


def _experts(h, slot_tok, block_e, nb_used, nb_max, layer, w_gate, w_up, w_down, bm):
    d = h.shape[1]
    ff = D_FF_EXPERT
    first, nxt = _expert_schedule(block_e, nb_used)
    hbm = pl.BlockSpec(memory_space=pl.ANY)
    grid_spec = pltpu.PrefetchScalarGridSpec(
        num_scalar_prefetch=5,
        grid=(nb_max,),
        in_specs=[hbm, hbm, hbm, hbm],
        out_specs=pl.BlockSpec((bm, d), lambda b, *_: (b, 0)),
        scratch_shapes=[pltpu.VMEM((bm, d), F32), pltpu.VMEM((bm, d), F32),
                        pltpu.VMEM((d, ff), F32), pltpu.VMEM((d, ff), F32), pltpu.VMEM((ff, d), F32),
                        pltpu.VMEM((d, ff), BF16), pltpu.VMEM((d, ff), BF16), pltpu.VMEM((ff, d), BF16),
                        pltpu.SemaphoreType.DMA((2,)), pltpu.SemaphoreType.DMA((3,))],
    )
    return pl.pallas_call(
        functools.partial(_expert_body, bm=bm, layer=layer),
        out_shape=jax.ShapeDtypeStruct((nb_max * bm, d), F32),
        grid_spec=grid_spec,
        compiler_params=pltpu.CompilerParams(dimension_semantics=("arbitrary",),
                                             vmem_limit_bytes=MOE_VMEM_LIMIT),
        name="experts",
    )(block_e, first, nxt, slot_tok, nb_used, h, w_gate, w_up, w_down)


def _combine_body(pos_ref, y_hbm, w_ref, x_ref, g2_ref, ng_ref, *rest, tc, s, final):
    if final:
        o_ref, yb_ref, sem = rest
    else:
        nsc_ref, nsh_ref, gw_ref, gb_ref, o_ref, h_ref, gate_ref, yb_ref, sem = rest
    i = pl.program_id(0)
    n = pl.num_programs(0)

    def start_gather(blk, slot):
        def go(r, c):
            for k in range(TOP_K):
                src = pos_ref[k * s + blk * tc + r]
                pltpu.make_async_copy(y_hbm.at[pl.ds(src, 1), :], yb_ref.at[slot, k, pl.ds(r, 1), :],
                                      sem.at[slot]).start()
            return c
        lax.fori_loop(0, tc, go, 0, unroll=MOE_GATHER_UNROLL // TOP_K)

    def wait_gather(slot):
        for k in range(TOP_K):
            pltpu.make_async_copy(y_hbm.at[pl.ds(0, tc), :], yb_ref.at[slot, k], sem.at[slot]).wait()

    slot = i % 2

    @pl.when(i == 0)
    def _():
        start_gather(0, 0)

    @pl.when(i + 1 < n)
    def _():
        start_gather(i + 1, 1 - slot)

    wait_gather(slot)
    w = w_ref[...]
    moe = w[:, 0:1] * yb_ref[slot, 0] + w[:, 1:2] * yb_ref[slot, 1]
    xn = x_ref[...] + g2_ref[...] * moe
    if final:
        o_ref[...] = (xn * lax.rsqrt(jnp.mean(xn * xn, axis=-1, keepdims=True) + EPS)) * ng_ref[...]
    else:
        o_ref[...] = xn
        h = _norm_mod(xn, ng_ref[...], nsc_ref[...], nsh_ref[...]).astype(h_ref.dtype)
        h_ref[...] = h
        gate_ref[...] = _gate_rows(h, gw_ref[0], gb_ref[...])


def _combine(ys, pos, wts, x, g2, tc, norm_g, next_layer=None):
    s, d = x.shape
    final = next_layer is None
    row = pl.BlockSpec((tc, d), lambda i, pos: (i, 0))
    vec = pl.BlockSpec((1, d), lambda i, pos: (0, 0))
    if final:
        extra_args, extra_specs = (), []
    else:
        norm_sc, norm_sh, w_in_t, gate_bias, layer = next_layer
        extra_args = (norm_sc, norm_sh, w_in_t, gate_bias)
        extra_specs = [vec, vec, _gate_w_spec(layer), pl.BlockSpec((GATE_COLS, 1), lambda i, pos: (0, 0))]
    grid_spec = pltpu.PrefetchScalarGridSpec(
        num_scalar_prefetch=1,
        grid=(s // tc,),
        in_specs=[pl.BlockSpec(memory_space=pl.ANY), pl.BlockSpec((tc, TOP_K), lambda i, pos: (i, 0)), row, vec, vec]
        + extra_specs,
        out_specs=row if final else (row, row, pl.BlockSpec((GATE_COLS, tc), lambda i, pos: (0, i))),
        scratch_shapes=[pltpu.VMEM((2, TOP_K, tc, d), F32), pltpu.SemaphoreType.DMA((2,))],
    )
    out_f32 = jax.ShapeDtypeStruct((s, d), F32)
    return pl.pallas_call(
        functools.partial(_combine_body, tc=tc, s=s, final=final),
        out_shape=out_f32 if final else (out_f32, jax.ShapeDtypeStruct((s, d), BF16),
                                         jax.ShapeDtypeStruct((GATE_COLS, s), F32)),
        grid_spec=grid_spec,
        compiler_params=_cparams(("arbitrary",)),
        name="moe_combine",
    )(pos, ys, wts, x, g2, norm_g, *extra_args)


def kernel(x, c, ada_w, ada_b, norm1_g, w_in, m_conv_w, m_conv_b, m_igate_b, m_fgate_b, m_norm_g, w_out,
           norm2_g, router_w, router_bias, w_gate, w_up, w_down, final_g):
    batch, s, d = x.shape
    assert batch == 1 and d == D_MODEL
    assert MLSTM_CHUNK == COMBINE_TC
    chunk = min(MLSTM_CHUNK, s)
    sb_tile = min(SB_TILE, s)
    tc = chunk
    xs = x.reshape(s, d)
    mod = _ada(c, ada_w, ada_b)
    router_wt = router_w.T
    w_in_t = jnp.swapaxes(w_in, 1, 2)
    mods = [[mod[l, :, i * d:(i + 1) * d] for i in range(6)] for l in range(DEPTH)]
    gate_bias = [jnp.concatenate([m_igate_b[l], m_fgate_b[l]]).reshape(GATE_COLS, 1) for l in range(DEPTH)]
    h1, gproc = _norm1(xs, norm1_g[0].reshape(1, d), mods[0][1], mods[0][0], w_in_t, gate_bias[0], 0, chunk)
    for l in range(DEPTH):
        sh1, sc1, g1, sh2, sc2, g2 = mods[l]
        p = _in_proj(h1, w_in_t, l)
        g_rows = gproc.reshape(2, M_HEADS, s).transpose(1, 0, 2)
        g_cols = g_rows.transpose(0, 2, 1)
        hm = _mlstm(p, m_conv_w[l], m_conv_b[l].reshape(1, -1), g_rows, g_cols, chunk)
        hs = _stick_breaking(p, sb_tile)
        xs = _out_proj(p, hm, hs, m_norm_g[l].reshape(1, d), xs, g1, w_out[l].astype(BF16))
        h2, eid, wts = _router(xs, norm2_g[l].reshape(1, d), sc2, sh2, router_wt, router_bias)
        dest, slot_tok, block_e, nb_used, nb_max = _dispatch(eid, MOE_BM)
        ys = _experts(h2, slot_tok, block_e, nb_used, nb_max, l, w_gate, w_up, w_down, MOE_BM)
        if l + 1 < DEPTH:
            xs, h1, gproc = _combine(ys, dest, wts.T, xs, g2, tc, norm1_g[l + 1].reshape(1, d),
                                     (mods[l + 1][1], mods[l + 1][0], w_in_t, gate_bias[l + 1], l + 1))
        else:
            out = _combine(ys, dest, wts.T, xs, g2, tc, final_g.reshape(1, d))
    return out.reshape(batch, s, d)
```

```python
import functools

import jax
import jax.numpy as jnp
from jax import lax
from jax.experimental import pallas as pl
from jax.experimental.pallas import tpu as pltpu

F32 = jnp.float32
BF16 = jnp.bfloat16

D_MODEL = 2048
DEPTH = 2
M_HEADS = 4
M_DV = D_MODEL // M_HEADS
M_DQK = M_DV // 2
CONV_K = 4
SB_HEADS = 16
SB_DH = D_MODEL // SB_HEADS
N_EXPERTS = 32
N_GROUPS = 4
EXPERTS_PER_GROUP = N_EXPERTS // N_GROUPS
TOP_K = 2
D_FF_EXPERT = 768
EPS = 1e-6
NEG = -1e30

M_QK_W = M_HEADS * M_DQK
M_V_W = M_HEADS * M_DV
SB_W = SB_HEADS * SB_DH
GATE_COL0 = 2 * M_QK_W + 2 * M_V_W
GATE_COLS = 2 * M_HEADS
P_MQ, P_MK, P_MV, P_MO = 0, M_QK_W, 2 * M_QK_W, 2 * M_QK_W + M_V_W
P_SQ = GATE_COL0
P_SK, P_SV, P_GM, P_GS = P_SQ + SB_W, P_SQ + 2 * SB_W, P_SQ + 3 * SB_W, P_SQ + 4 * SB_W
P_W = P_SQ + 5 * SB_W

LANES = 128
SUBLANES = 8
VMEM_LIMIT = 56 * 1024 * 1024

MLSTM_CHUNK = 256
SB_TILE = 256
MOE_BM = 512
MOE_FF_CHUNK = 256
MOE_GATHER_UNROLL = 8
COMBINE_TC = 256
ADA_TN = 1024
PROJ_TM = 1024
ROW_TM = 512


def _cparams(sem):
    return pltpu.CompilerParams(dimension_semantics=sem, vmem_limit_bytes=VMEM_LIMIT)


def _ada_body(c_ref, w_ref, b_ref, o_ref):
    c = c_ref[...]
    cs = c * jax.nn.sigmoid(c)
    for j in range(w_ref.shape[2] // LANES):
        sl = slice(j * LANES, (j + 1) * LANES)
        o_ref[0, :, sl] = jnp.sum(w_ref[0, :, sl] * cs, axis=0, keepdims=True) + b_ref[0, :, sl]


def _ada(c, ada_w, ada_b):
    depth, d, n = ada_w.shape
    tn = ADA_TN
    cb = jnp.broadcast_to(c.reshape(d, 1), (d, LANES))
    return pl.pallas_call(
        _ada_body,
        out_shape=jax.ShapeDtypeStruct((depth, 1, n), F32),
        grid=(depth, n // tn),
        in_specs=[pl.BlockSpec((d, LANES), lambda l, j: (0, 0)),
                  pl.BlockSpec((1, d, tn), lambda l, j: (l, 0, j)),
                  pl.BlockSpec((1, 1, tn), lambda l, j: (l, 0, j))],
        out_specs=pl.BlockSpec((1, 1, tn), lambda l, j: (l, 0, j)),
        compiler_params=_cparams(("arbitrary", "arbitrary")),
        name="ada_mod",
    )(cb, ada_w, ada_b.reshape(depth, 1, n))


def _norm_mod(x, g, sc, sh):
    y = x * lax.rsqrt(jnp.mean(x * x, axis=-1, keepdims=True) + EPS)
    return (y * g) * (1.0 + sc) + sh


def _gate_rows(h, w, b):
    v = lax.dot_general(w.astype(BF16), h, (((1,), (1,)), ((), ())), preferred_element_type=F32) + b
    n = v.shape[1]
    logf = jnp.minimum(v, 0.0) - jnp.log(1.0 + jnp.exp(-jnp.abs(v)))
    upper = (lax.broadcasted_iota(jnp.int32, (n, n), 0)
             <= lax.broadcasted_iota(jnp.int32, (n, n), 1)).astype(F32)
    csum = jnp.dot(logf, upper, precision=lax.Precision.HIGHEST, preferred_element_type=F32)
    row = lax.broadcasted_iota(jnp.int32, v.shape, 0)
    return jnp.where(row < M_HEADS, v, csum)


def _gate_w_spec(layer):
    return pl.BlockSpec((1, GATE_COLS, D_MODEL), lambda *_: (layer, GATE_COL0 // GATE_COLS, 0))


def _norm1_body(x_ref, g_ref, sc_ref, sh_ref, gw_ref, gb_ref, o_ref, gate_ref):
    h = _norm_mod(x_ref[...], g_ref[...], sc_ref[...], sh_ref[...]).astype(o_ref.dtype)
    o_ref[...] = h
    gate_ref[...] = _gate_rows(h, gw_ref[0], gb_ref[...])


def _norm1(x, g, sc, sh, w_in_t, gate_bias, layer, chunk):
    s, d = x.shape
    vec = pl.BlockSpec((1, d), lambda i: (0, 0))
    return pl.pallas_call(
        _norm1_body,
        out_shape=(jax.ShapeDtypeStruct((s, d), BF16), jax.ShapeDtypeStruct((GATE_COLS, s), F32)),
        grid=(s // chunk,),
        in_specs=[pl.BlockSpec((chunk, d), lambda i: (i, 0)), vec, vec, vec, _gate_w_spec(layer),
                  pl.BlockSpec((GATE_COLS, 1), lambda i: (0, 0))],
        out_specs=(pl.BlockSpec((chunk, d), lambda i: (i, 0)), pl.BlockSpec((GATE_COLS, chunk), lambda i: (0, i))),
        compiler_params=_cparams(("arbitrary",)),
        name="norm1",
    )(x, g, sc, sh, w_in_t, gate_bias)


IN_PROJ_TN = 1024


def _in_proj_body(h_ref, wt_ref, p_ref, w_ref):
    @pl.when(pl.program_id(1) == 0)
    def _():
        w_ref[...] = wt_ref[0].astype(BF16)

    p_ref[...] = lax.dot_general(h_ref[...], w_ref[...], (((1,), (1,)), ((), ())),
                                 preferred_element_type=F32).astype(p_ref.dtype)


def _in_proj(h, w_in_t, layer):
    s, d = h.shape
    tm, tn = min(PROJ_TM, s), IN_PROJ_TN

    def w_rows(j, i):
        tiles = j * (tn // GATE_COLS) + jnp.where(j >= GATE_COL0 // tn, 1, 0)
        return (layer, pl.multiple_of(tiles * GATE_COLS, GATE_COLS), 0)

    return pl.pallas_call(
        _in_proj_body,
        out_shape=jax.ShapeDtypeStruct((s, P_W), BF16),
        grid=(P_W // tn, s // tm),
        in_specs=[pl.BlockSpec((tm, d), lambda j, i: (i, 0)),
                  pl.BlockSpec((pl.Element(1), pl.Element(tn), pl.Element(d)), w_rows)],
        out_specs=pl.BlockSpec((tm, tn), lambda j, i: (i, j)),
        scratch_shapes=[pltpu.VMEM((tn, d), BF16)],
        compiler_params=_cparams(("arbitrary", "arbitrary")),
        name="in_proj",
    )(h, w_in_t)


def _conv_silu(x, tail, w, b):
    row8 = lax.broadcasted_iota(jnp.int32, tail.shape, 0)
    y = b + w[CONV_K - 1:CONV_K, :] * x
    for d in range(1, CONV_K):
        rolled = pltpu.roll(x, d, 0)
        head = jnp.where(row8 < d, pltpu.roll(tail, d, 0), rolled[:SUBLANES])
        xd = jnp.concatenate([head, rolled[SUBLANES:]], axis=0)
        y = y + w[CONV_K - 1 - d:CONV_K - d, :] * xd
    return y * jax.nn.sigmoid(y)


def _mlstm_body(q_ref, k_ref, v_ref, cw_ref, cb_ref, gr_ref, gc_ref, o_ref, c_ref, m_ref, tq_ref, tk_ref):
    @pl.when(pl.program_id(0) == 0)
    def _():
        c_ref[...] = jnp.zeros_like(c_ref)
        m_ref[...] = jnp.zeros_like(m_ref)
        tq_ref[...] = jnp.zeros_like(tq_ref)
        tk_ref[...] = jnp.zeros_like(tk_ref)

    n = q_ref.shape[0]
    causal = (lax.broadcasted_iota(jnp.int32, (n, n), 1) <= lax.broadcasted_iota(jnp.int32, (n, n), 0))
    for hd in range(M_HEADS):
        qc = slice(hd * M_DQK, (hd + 1) * M_DQK)
        kc = slice(M_QK_W + hd * M_DQK, M_QK_W + (hd + 1) * M_DQK)
        vc = slice(hd * M_DV, (hd + 1) * M_DV)
        xq = q_ref[:, qc].astype(F32)
        xk = k_ref[:, qc].astype(F32)
        q = _conv_silu(xq, tq_ref[:, qc], cw_ref[:, qc], cb_ref[:, qc]) * (M_DQK ** -0.5)
        k = _conv_silu(xk, tk_ref[:, qc], cw_ref[:, kc], cb_ref[:, kc])
        tq_ref[:, qc] = xq[n - SUBLANES:, :]
        tk_ref[:, qc] = xk[n - SUBLANES:, :]
        qb = q.astype(BF16)
        kb = k.astype(BF16)
        vext = jnp.concatenate([v_ref[:, vc], jnp.ones((n, LANES), BF16)], axis=1)

        gr = gr_ref[hd]
        gc = gc_ref[hd]
        i_row, g_row = gr[0:1, :], gr[1:2, :]
        i_col, g_col = gc[:, 0:1], gc[:, 1:2]
        m = m_ref[hd]

        dmat = jnp.where(causal, g_col - g_row + i_row, NEG)
        inter = g_col + m
        m_row = jnp.maximum(inter, jnp.max(dmat, axis=-1, keepdims=True))
        w_intra = jnp.exp(dmat - m_row)
        a_inter = jnp.exp(inter - m_row)
        s_qk = lax.dot_general(qb, kb, (((1,), (1,)), ((), ())), preferred_element_type=F32) * w_intra
        state = c_ref[hd]
        tot = (jnp.dot(s_qk.astype(BF16), vext, preferred_element_type=F32)
               + a_inter * jnp.dot(qb, state.astype(BF16), preferred_element_type=F32))
        num = tot[:, :M_DV]
        den = tot[:, M_DV:M_DV + 1]
        h = num / jnp.maximum(jnp.abs(den), jnp.exp(-m_row))
        h = h * lax.rsqrt(jnp.mean(h * h, axis=-1, keepdims=True) + EPS)
        o_ref[:, vc] = h.astype(o_ref.dtype)

        g_last = g_col[n - 1:n, :]
        w_k = g_last - g_col + i_col
        m_new = jnp.maximum(g_last + m, jnp.max(w_k, axis=0, keepdims=True))
        decay = jnp.exp(g_last + m - m_new)
        kw = (k * jnp.exp(w_k - m_new)).astype(BF16)
        upd = lax.dot_general(kw, vext, (((0,), (0,)), ((), ())), preferred_element_type=F32)
        c_ref[hd] = decay * state + upd
        m_ref[hd] = m_new


def _mlstm(p, conv_w, conv_b, g_rows, g_cols, chunk):
    s = p.shape[0]
    return pl.pallas_call(
        _mlstm_body,
        out_shape=jax.ShapeDtypeStruct((s, M_V_W), BF16),
        grid=(s // chunk,),
        in_specs=[pl.BlockSpec((chunk, M_QK_W), lambda c: (c, P_MQ // M_QK_W)),
                  pl.BlockSpec((chunk, M_QK_W), lambda c: (c, P_MK // M_QK_W)),
                  pl.BlockSpec((chunk, M_V_W), lambda c: (c, P_MV // M_V_W)),
                  pl.BlockSpec((CONV_K, 2 * M_QK_W), lambda c: (0, 0)),
                  pl.BlockSpec((1, 2 * M_QK_W), lambda c: (0, 0)),
                  pl.BlockSpec((M_HEADS, 2, chunk), lambda c: (0, 0, c)),
                  pl.BlockSpec((M_HEADS, chunk, 2), lambda c: (0, c, 0))],
        out_specs=pl.BlockSpec((chunk, M_V_W), lambda c: (c, 0)),
        scratch_shapes=[pltpu.VMEM((M_HEADS, M_DQK, M_DV + LANES), F32), pltpu.VMEM((M_HEADS, 1, 1), F32),
                        pltpu.VMEM((SUBLANES, M_QK_W), F32), pltpu.VMEM((SUBLANES, M_QK_W), F32)],
        compiler_params=_cparams(("arbitrary",)),
        name="mlstm",
    )(p, p, p, conv_w, conv_b, g_rows, g_cols)


SB_UNDERFLOW = 105.0
SB_HEADS_PER_STEP = 4


def _sb_body(q_ref, k_ref, v_ref, o_ref, *, tile):
    i = pl.program_id(1)
    rows = lax.broadcasted_iota(jnp.int32, (tile, tile), 0)
    cols = lax.broadcasted_iota(jnp.int32, (tile, tile), 1)
    later = (rows > cols).astype(BF16)
    diag_mask = cols < rows

    def head(hh, with_prev):
        lanes = slice(hh * SB_DH, (hh + 1) * SB_DH)
        q = (q_ref[:, lanes].astype(F32) * (SB_DH ** -0.5)).astype(BF16)

        def logits(j, masked):
            kb = k_ref[pl.ds(pl.multiple_of(j * tile, tile), tile), lanes]
            z = lax.dot_general(q, kb, (((1,), (1,)), ((), ())), preferred_element_type=F32)
            sp = jnp.maximum(z, 0.0) + jnp.log(1.0 + jnp.exp(-jnp.abs(z)))
            if masked:
                sp = jnp.where(diag_mask, sp, 0.0)
            within = jnp.dot(sp.astype(BF16), later, preferred_element_type=F32)
            return z, sp, within, jnp.sum(sp, axis=-1, keepdims=True)

        def weighted(j, z, sp, within, rsum, masked):
            vb = v_ref[pl.ds(pl.multiple_of(j * tile, tile), tile), lanes]
            a = jnp.exp((z - sp) - within - rsum)
            if masked:
                a = jnp.where(diag_mask, a, 0.0)
            return jnp.dot(a.astype(BF16), vb, preferred_element_type=F32)

        zd, spd, wd, sd = logits(i, True)
        if not with_prev:
            return weighted(i, zd, spd, wd, jnp.zeros((tile, 1), F32), True), None, None
        zp, spp, wp, sprev = logits(i - 1, False)
        acc = (weighted(i, zd, spd, wd, jnp.zeros((tile, 1), F32), True)
               + weighted(i - 1, zp, spp, wp, sd, False))

        def cond(c):
            return jnp.logical_and(c[0] >= 0, jnp.min(c[1]) <= SB_UNDERFLOW)

        def body(c):
            j, rs, ac = c
            z, sp, w, sj = logits(j, False)
            return j - 1, rs + sj, ac + weighted(j, z, sp, w, rs, False)

        def scan_earlier(rs, ac):
            return lax.while_loop(cond, body, (i - 2, rs, ac))[2]

        return acc, sd + sprev, scan_earlier

    def store(hh, acc):
        o_ref[:, hh * SB_DH:(hh + 1) * SB_DH] = acc.astype(o_ref.dtype)

    @pl.when(i == 0)
    def _():
        for hh in range(SB_HEADS_PER_STEP):
            store(hh, head(hh, False)[0])

    @pl.when(i > 0)
    def _():
        firsts = [head(hh, True) for hh in range(SB_HEADS_PER_STEP)]
        for hh, (acc, rsum, scan_earlier) in enumerate(firsts):
            store(hh, scan_earlier(rsum, acc))


def _stick_breaking(p, tile):
    s = p.shape[0]
    w = SB_HEADS_PER_STEP * SB_DH
    cq, ck, cv = P_SQ // w, P_SK // w, P_SV // w
    return pl.pallas_call(
        functools.partial(_sb_body, tile=tile),
        out_shape=jax.ShapeDtypeStruct((s, SB_W), BF16),
        grid=(SB_HEADS // SB_HEADS_PER_STEP, s // tile),
        in_specs=[pl.BlockSpec((tile, w), lambda h, i: (i, cq + h)),
                  pl.BlockSpec((s, w), lambda h, i: (0, ck + h)),
                  pl.BlockSpec((s, w), lambda h, i: (0, cv + h))],
        out_specs=pl.BlockSpec((tile, w), lambda h, i: (i, h)),
        compiler_params=_cparams(("arbitrary", "arbitrary")),
        name="stick_breaking",
    )(p, p, p)


def _out_proj_body(mo_ref, gm_ref, gs_ref, hm_ref, hs_ref, ng_ref, x_ref, g1_ref, w_ref, o_ref):
    sig = jax.nn.sigmoid
    hm = sig(mo_ref[...].astype(F32)) * (hm_ref[...].astype(F32) * ng_ref[...])
    y = sig(gm_ref[...].astype(F32)) * hm + sig(gs_ref[...].astype(F32)) * hs_ref[...].astype(F32)
    o_ref[...] = x_ref[...] + g1_ref[...] * jnp.dot(y.astype(BF16), w_ref[...], preferred_element_type=F32)


def _out_proj(p, hm, hs, norm_g, x, g1, w_out):
    s, d = x.shape
    tm = min(ROW_TM, s)
    row = lambda blk: pl.BlockSpec((tm, d), lambda i: (i, blk))
    vec = pl.BlockSpec((1, d), lambda i: (0, 0))
    return pl.pallas_call(
        _out_proj_body,
        out_shape=jax.ShapeDtypeStruct((s, d), F32),
        grid=(s // tm,),
        in_specs=[row(P_MO // d), row(P_GM // d), row(P_GS // d), row(0), row(0), vec, row(0), vec,
                  pl.BlockSpec((d, d), lambda i: (0, 0), pipeline_mode=pl.Buffered(1))],
        out_specs=row(0),
        compiler_params=_cparams(("arbitrary",)),
        name="out_proj",
    )(p, p, p, hm, hs, norm_g, x, g1, w_out)


def _router_body(x_ref, g_ref, sc_ref, sh_ref, rwt_ref, rb_ref, h_ref, eid_ref, wt_ref):
    h = _norm_mod(x_ref[...], g_ref[...], sc_ref[...], sh_ref[...])
    h_ref[...] = h
    nt = (((1,), (1,)), ((), ()))
    h_hi = h.astype(BF16)
    h_lo = (h - h_hi.astype(F32)).astype(BF16)
    rw = rwt_ref[...]
    rw_hi = rw.astype(BF16)
    rw_lo = (rw - rw_hi.astype(F32)).astype(BF16)
    logits = (lax.dot_general(rw_hi, h_hi, nt, preferred_element_type=F32)
              + lax.dot_general(rw_hi, h_lo, nt, preferred_element_type=F32)
              + lax.dot_general(rw_lo, h_hi, nt, preferred_element_type=F32))
    scores = jax.nn.sigmoid(logits)
    sel = scores + rb_ref[...]
    t = sel.shape[1]
    idx = lax.broadcasted_iota(jnp.int32, (EXPERTS_PER_GROUP, t), 0)
    best = None
    for g in range(N_GROUPS):
        sl = slice(g * EXPERTS_PER_GROUP, (g + 1) * EXPERTS_PER_GROUP)
        s, sc = sel[sl], scores[sl]
        m1 = jnp.max(s, axis=0, keepdims=True)
        i1 = jnp.min(jnp.where(s == m1, idx, EXPERTS_PER_GROUP), axis=0, keepdims=True)
        s2 = jnp.where(idx == i1, -jnp.inf, s)
        m2 = jnp.max(s2, axis=0, keepdims=True)
        i2 = jnp.min(jnp.where(s2 == m2, idx, EXPERTS_PER_GROUP), axis=0, keepdims=True)
        w1 = jnp.sum(jnp.where(idx == i1, sc, 0.0), axis=0, keepdims=True)
        w2 = jnp.sum(jnp.where(idx == i2, sc, 0.0), axis=0, keepdims=True)
        cand = (m1 + m2, i1 + g * EXPERTS_PER_GROUP, i2 + g * EXPERTS_PER_GROUP, w1, w2)
        if best is None:
            best = cand
        else:
            better = cand[0] > best[0]
            best = tuple(jnp.where(better, c, b) for c, b in zip(cand, best))
    _, e1, e2, w1, w2 = best
    wsum = w1 + w2
    eid_ref[...] = jnp.concatenate([e1, e2], axis=0)
    wt_ref[...] = jnp.concatenate([w1 / wsum, w2 / wsum], axis=0)


def _router(x, g, sc, sh, router_wt, router_bias):
    s, d = x.shape
    tm = min(ROW_TM, s)
    vec = pl.BlockSpec((1, d), lambda i: (0, 0))
    return pl.pallas_call(
        _router_body,
        out_shape=(jax.ShapeDtypeStruct((s, d), F32), jax.ShapeDtypeStruct((TOP_K, s), jnp.int32),
                   jax.ShapeDtypeStruct((TOP_K, s), F32)),
        grid=(s // tm,),
        in_specs=[pl.BlockSpec((tm, d), lambda i: (i, 0)), vec, vec, vec,
                  pl.BlockSpec((N_EXPERTS, d), lambda i: (0, 0)),
                  pl.BlockSpec((N_EXPERTS, 1), lambda i: (0, 0))],
        out_specs=(pl.BlockSpec((tm, d), lambda i: (i, 0)), pl.BlockSpec((TOP_K, tm), lambda i: (0, i)),
                   pl.BlockSpec((TOP_K, tm), lambda i: (0, i))),
        compiler_params=_cparams(("arbitrary",)),
        name="router",
    )(x, g, sc, sh, router_wt, router_bias.reshape(N_EXPERTS, 1))


def _dispatch(eid, bm):
    s = eid.shape[1]
    n_assign = TOP_K * s
    flat_e = eid.reshape(-1)
    onehot = (flat_e[:, None] == jnp.arange(N_EXPERTS, dtype=jnp.int32)[None, :]).astype(jnp.int32)
    csum = jnp.cumsum(onehot, axis=0)
    rank = jnp.sum(csum * onehot, axis=1) - 1
    counts = csum[-1]
    padded = (counts + bm - 1) // bm * bm
    pends = jnp.cumsum(padded)
    dest = (pends - padded)[flat_e] + rank
    nb_max = pl.cdiv(n_assign, bm) + N_EXPERTS
    flat_t = jnp.tile(jnp.arange(s, dtype=jnp.int32), TOP_K)
    slot_tok = (jnp.arange(nb_max * bm, dtype=jnp.int32) % s).at[dest].set(flat_t)
    nb_used = (pends[-1] // bm).astype(jnp.int32).reshape(1)
    block_start = jnp.arange(nb_max, dtype=jnp.int32) * bm
    block_e = jnp.minimum(jnp.sum((pends[None, :] <= block_start[:, None]).astype(jnp.int32), axis=1),
                          N_EXPERTS - 1)
    return dest.astype(jnp.int32), slot_tok, block_e, nb_used, nb_max


def _expert_schedule(block_e, nb_used):
    nb_max = block_e.shape[0]
    idx = jnp.arange(nb_max, dtype=jnp.int32)
    used = idx < nb_used[0]
    prev_e = jnp.concatenate([jnp.full((1,), -1, jnp.int32), block_e[:-1]])
    first = (used & (block_e != prev_e)).astype(jnp.int32)
    pos = jnp.where(first == 1, idx, nb_max)
    after = jnp.concatenate([lax.cummin(pos[::-1])[::-1][1:], jnp.full((1,), nb_max, jnp.int32)])
    nxt = jnp.where(after < nb_max, block_e[jnp.minimum(after, nb_max - 1)], -1)
    return first, nxt.astype(jnp.int32)


def _expert_body(be_ref, first_ref, nxt_ref, st_ref, nu_ref, h_hbm, wg_hbm, wu_hbm, wd_hbm, y_ref,
                 xa_ref, xb_ref, wg_st, wu_st, wd_st, wg_bf, wu_bf, wd_bf, gsem, wsem, *, bm, layer):
    b = pl.program_id(0)
    nb = nu_ref[0]
    n_chunks = D_FF_EXPERT // MOE_FF_CHUNK

    def row_copy(tok, r, buf, si):
        return pltpu.make_async_copy(h_hbm.at[pl.ds(tok, 1), :], buf.at[pl.ds(r, 1), :], gsem.at[si])

    def wait_gather(buf, si):
        pltpu.make_async_copy(h_hbm.at[pl.ds(0, bm), :], buf, gsem.at[si]).wait()

    def weight_copies(e):
        return (pltpu.make_async_copy(wg_hbm.at[layer, e], wg_st, wsem.at[0]),
                pltpu.make_async_copy(wu_hbm.at[layer, e], wu_st, wsem.at[1]),
                pltpu.make_async_copy(wd_hbm.at[layer, e], wd_st, wsem.at[2]))

    @pl.when(b == 0)
    def _():
        for cp in weight_copies(be_ref[0]):
            cp.start()

        def go(r, c):
            row_copy(st_ref[r], r, xa_ref, 0).start()
            return c
        lax.fori_loop(0, bm, go, 0, unroll=MOE_GATHER_UNROLL)

    @pl.when((b < nb) & (first_ref[b] == 1))
    def _():
        for cp in weight_copies(be_ref[b]):
            cp.wait()
        for c in range(n_chunks):
            cs = slice(c * MOE_FF_CHUNK, (c + 1) * MOE_FF_CHUNK)
            wg_bf[:, cs] = wg_st[:, cs].astype(BF16)
            wu_bf[:, cs] = wu_st[:, cs].astype(BF16)
            wd_bf[cs, :] = wd_st[cs, :].astype(BF16)

        @pl.when(nxt_ref[b] >= 0)
        def _():
            for cp in weight_copies(nxt_ref[b]):
                cp.start()

    def step(cur, cur_si, nxt_buf, nxt_si):
        wait_gather(cur, cur_si)
        base = jnp.minimum(b + 1, nb - 1) * bm
        x = cur[...].astype(BF16)
        y = None
        for c in range(n_chunks):
            for r in range(c * bm // n_chunks, (c + 1) * bm // n_chunks):
                row_copy(st_ref[base + r], r, nxt_buf, nxt_si).start(priority=r % 2)
            cs = slice(c * MOE_FF_CHUNK, (c + 1) * MOE_FF_CHUNK)
            gate = jnp.dot(x, wg_bf[:, cs], preferred_element_type=F32)
            up = jnp.dot(x, wu_bf[:, cs], preferred_element_type=F32)
            mid = (gate * jax.nn.sigmoid(gate) * up).astype(BF16)
            part = jnp.dot(mid, wd_bf[cs, :], preferred_element_type=F32)
            y = part if y is None else y + part
        y_ref[...] = y

        @pl.when(b == nb - 1)
        def _():
            wait_gather(nxt_buf, nxt_si)

    @pl.when((b < nb) & (b % 2 == 0))
    def _():
        step(xa_ref, 0, xb_ref, 1)

    @pl.when((b < nb) & (b % 2 == 1))
    def _():
        step(xb_ref, 1, xa_ref, 0)

    @pl.when(b >= nb)
    def _():
        y_ref[...] = jnp.zeros_like(y_ref)


def _experts(h, slot_tok, block_e, nb_used, nb_max, layer, w_gate, w_up, w_down, bm):
    d = h.shape[1]
    ff = D_FF_EXPERT
    first, nxt = _expert_schedule(block_e, nb_used)
    hbm = pl.BlockSpec(memory_space=pl.ANY)
    grid_spec = pltpu.PrefetchScalarGridSpec(
        num_scalar_prefetch=5,
        grid=(nb_max,),
        in_specs=[hbm, hbm, hbm, hbm],
        out_specs=pl.BlockSpec((bm, d), lambda b, *_: (b, 0)),
        scratch_shapes=[pltpu.VMEM((bm, d), F32), pltpu.VMEM((bm, d), F32),
                        pltpu.VMEM((d, ff), F32), pltpu.VMEM((d, ff), F32), pltpu.VMEM((ff, d), F32),
                        pltpu.VMEM((d, ff), BF16), pltpu.VMEM((d, ff), BF16), pltpu.VMEM((ff, d), BF16),
                        pltpu.SemaphoreType.DMA((2,)), pltpu.SemaphoreType.DMA((3,))],
    )
    return pl.pallas_call(
        functools.partial(_expert_body, bm=bm, layer=layer),
        out_shape=jax.ShapeDtypeStruct((nb_max * bm, d), F32),
        grid_spec=grid_spec,
        compiler_params=pltpu.CompilerParams(dimension_semantics=("arbitrary",),
                                             vmem_limit_bytes=VMEM_LIMIT),
        name="experts",
    )(block_e, first, nxt, slot_tok, nb_used, h, w_gate, w_up, w_down)


def _combine_body(pos_ref, y_hbm, w_ref, x_ref, g2_ref, ng_ref, *rest, tc, s, final):
    if final:
        o_ref, yb_ref, sem = rest
    else:
        nsc_ref, nsh_ref, gw_ref, gb_ref, o_ref, h_ref, gate_ref, yb_ref, sem = rest
    i = pl.program_id(0)
    n = pl.num_programs(0)

    def start_gather(blk, slot):
        for k in range(TOP_K):
            for r in range(tc):
                src = pos_ref[k * s + blk * tc + r]
                pltpu.make_async_copy(y_hbm.at[pl.ds(src, 1), :], yb_ref.at[slot, k, pl.ds(r, 1), :],
                                      sem.at[slot]).start(priority=k)

    def wait_gather(slot):
        for k in range(TOP_K):
            pltpu.make_async_copy(y_hbm.at[pl.ds(0, tc), :], yb_ref.at[slot, k], sem.at[slot]).wait()

    slot = i % 2

    @pl.when(i == 0)
    def _():
        start_gather(0, 0)

    @pl.when(i + 1 < n)
    def _():
        start_gather(i + 1, 1 - slot)

    wait_gather(slot)
    w = w_ref[...]
    moe = w[:, 0:1] * yb_ref[slot, 0] + w[:, 1:2] * yb_ref[slot, 1]
    xn = x_ref[...] + g2_ref[...] * moe
    if final:
        o_ref[...] = (xn * lax.rsqrt(jnp.mean(xn * xn, axis=-1, keepdims=True) + EPS)) * ng_ref[...]
    else:
        o_ref[...] = xn
        h = _norm_mod(xn, ng_ref[...], nsc_ref[...], nsh_ref[...]).astype(h_ref.dtype)
        h_ref[...] = h
        gate_ref[...] = _gate_rows(h, gw_ref[0], gb_ref[...])


def _combine(ys, pos, wts, x, g2, tc, norm_g, next_layer=None):
    s, d = x.shape
    final = next_layer is None
    row = pl.BlockSpec((tc, d), lambda i, pos: (i, 0))
    vec = pl.BlockSpec((1, d), lambda i, pos: (0, 0))
    if final:
        extra_args, extra_specs = (), []
    else:
        norm_sc, norm_sh, w_in_t, gate_bias, layer = next_layer
        extra_args = (norm_sc, norm_sh, w_in_t, gate_bias)
        extra_specs = [vec, vec, _gate_w_spec(layer), pl.BlockSpec((GATE_COLS, 1), lambda i, pos: (0, 0))]
    grid_spec = pltpu.PrefetchScalarGridSpec(
        num_scalar_prefetch=1,
        grid=(s // tc,),
        in_specs=[pl.BlockSpec(memory_space=pl.ANY), pl.BlockSpec((tc, TOP_K), lambda i, pos: (i, 0)), row, vec, vec]
        + extra_specs,
        out_specs=row if final else (row, row, pl.BlockSpec((GATE_COLS, tc), lambda i, pos: (0, i))),
        scratch_shapes=[pltpu.VMEM((2, TOP_K, tc, d), F32), pltpu.SemaphoreType.DMA((2,))],
    )
    out_f32 = jax.ShapeDtypeStruct((s, d), F32)
    return pl.pallas_call(
        functools.partial(_combine_body, tc=tc, s=s, final=final),
        out_shape=out_f32 if final else (out_f32, jax.ShapeDtypeStruct((s, d), BF16),
                                         jax.ShapeDtypeStruct((GATE_COLS, s), F32)),
        grid_spec=grid_spec,
        compiler_params=_cparams(("arbitrary",)),
        name="moe_combine",
    )(pos, ys, wts, x, g2, norm_g, *extra_args)


def kernel(x, c, ada_w, ada_b, norm1_g, w_in, m_conv_w, m_conv_b, m_igate_b, m_fgate_b, m_norm_g, w_out,
           norm2_g, router_w, router_bias, w_gate, w_up, w_down, final_g):
    batch, s, d = x.shape
    assert batch == 1 and d == D_MODEL
    assert MLSTM_CHUNK == COMBINE_TC
    chunk = min(MLSTM_CHUNK, s)
    sb_tile = min(SB_TILE, s)
    tc = chunk
    xs = x.reshape(s, d)
    mod = _ada(c, ada_w, ada_b)
    router_wt = router_w.T
    w_in_t = jnp.swapaxes(w_in, 1, 2)
    mods = [[mod[l, :, i * d:(i + 1) * d] for i in range(6)] for l in range(DEPTH)]
    gate_bias = [jnp.concatenate([m_igate_b[l], m_fgate_b[l]]).reshape(GATE_COLS, 1) for l in range(DEPTH)]
    h1, gproc = _norm1(xs, norm1_g[0].reshape(1, d), mods[0][1], mods[0][0], w_in_t, gate_bias[0], 0, chunk)
    for l in range(DEPTH):
        sh1, sc1, g1, sh2, sc2, g2 = mods[l]
        p = _in_proj(h1, w_in_t, l)
        g_rows = gproc.reshape(2, M_HEADS, s).transpose(1, 0, 2)
        g_cols = g_rows.transpose(0, 2, 1)
        hm = _mlstm(p, m_conv_w[l], m_conv_b[l].reshape(1, -1), g_rows, g_cols, chunk)
        hs = _stick_breaking(p, sb_tile)
        xs = _out_proj(p, hm, hs, m_norm_g[l].reshape(1, d), xs, g1, w_out[l].astype(BF16))
        h2, eid, wts = _router(xs, norm2_g[l].reshape(1, d), sc2, sh2, router_wt, router_bias)
        dest, slot_tok, block_e, nb_used, nb_max = _dispatch(eid, MOE_BM)
        ys = _experts(h2, slot_tok, block_e, nb_used, nb_max, l, w_gate, w_up, w_down, MOE_BM)
        if l + 1 < DEPTH:
            xs, h1, gproc = _combine(ys, dest, wts.T, xs, g2, tc, norm1_g[l + 1].reshape(1, d),
                                     (mods[l + 1][1], mods[l + 1][0], w_in_t, gate_bias[l + 1], l + 1))
        else:
            out = _combine(ys, dest, wts.T, xs, g2, tc, final_g.reshape(1, d))
    return out.reshape(batch, s, d)
```

```python
import functools

import jax
import jax.numpy as jnp
from jax import lax
from jax.experimental import pallas as pl
from jax.experimental.pallas import tpu as pltpu

F32 = jnp.float32
BF16 = jnp.bfloat16

D_MODEL = 2048
DEPTH = 2
M_HEADS = 4
M_DV = D_MODEL // M_HEADS
M_DQK = M_DV // 2
CONV_K = 4
SB_HEADS = 16
SB_DH = D_MODEL // SB_HEADS
N_EXPERTS = 32
N_GROUPS = 4
EXPERTS_PER_GROUP = N_EXPERTS // N_GROUPS
TOP_K = 2
D_FF_EXPERT = 768
EPS = 1e-6
NEG = -1e30

M_QK_W = M_HEADS * M_DQK
M_V_W = M_HEADS * M_DV
SB_W = SB_HEADS * SB_DH
GATE_COL0 = 2 * M_QK_W + 2 * M_V_W
GATE_COLS = 2 * M_HEADS
P_MQ, P_MK, P_MV, P_MO = 0, M_QK_W, 2 * M_QK_W, 2 * M_QK_W + M_V_W
P_SQ = GATE_COL0
P_SK, P_SV, P_GM, P_GS = P_SQ + SB_W, P_SQ + 2 * SB_W, P_SQ + 3 * SB_W, P_SQ + 4 * SB_W
P_W = P_SQ + 5 * SB_W

LANES = 128
SUBLANES = 8
VMEM_LIMIT = 56 * 1024 * 1024

MLSTM_CHUNK = 256
SB_TILE = 256
MOE_BM = 512
MOE_FF_CHUNK = 256
MOE_GATHER_UNROLL = 8
COMBINE_TC = 256
ADA_TN = 1024
PROJ_TM = 2048
ROW_TM = 512


def _cparams(sem):
    return pltpu.CompilerParams(dimension_semantics=sem, vmem_limit_bytes=VMEM_LIMIT)


def _ada_body(c_ref, w_ref, b_ref, o_ref):
    c = c_ref[...]
    cs = c * jax.nn.sigmoid(c)
    for j in range(w_ref.shape[2] // LANES):
        sl = slice(j * LANES, (j + 1) * LANES)
        o_ref[0, :, sl] = jnp.sum(w_ref[0, :, sl] * cs, axis=0, keepdims=True) + b_ref[0, :, sl]


def _ada(c, ada_w, ada_b):
    depth, d, n = ada_w.shape
    tn = ADA_TN
    cb = jnp.broadcast_to(c.reshape(d, 1), (d, LANES))
    return pl.pallas_call(
        _ada_body,
        out_shape=jax.ShapeDtypeStruct((depth, 1, n), F32),
        grid=(depth, n // tn),
        in_specs=[pl.BlockSpec((d, LANES), lambda l, j: (0, 0)),
                  pl.BlockSpec((1, d, tn), lambda l, j: (l, 0, j)),
                  pl.BlockSpec((1, 1, tn), lambda l, j: (l, 0, j))],
        out_specs=pl.BlockSpec((1, 1, tn), lambda l, j: (l, 0, j)),
        compiler_params=_cparams(("arbitrary", "arbitrary")),
        name="ada_mod",
    )(cb, ada_w, ada_b.reshape(depth, 1, n))


def _norm_mod(x, g, sc, sh):
    y = x * lax.rsqrt(jnp.mean(x * x, axis=-1, keepdims=True) + EPS)
    return (y * g) * (1.0 + sc) + sh


def _gate_rows(h, w, b):
    v = lax.dot_general(w.astype(BF16), h, (((1,), (1,)), ((), ())), preferred_element_type=F32) + b
    n = v.shape[1]
    logf = jnp.minimum(v, 0.0) - jnp.log(1.0 + jnp.exp(-jnp.abs(v)))
    upper = (lax.broadcasted_iota(jnp.int32, (n, n), 0)
             <= lax.broadcasted_iota(jnp.int32, (n, n), 1)).astype(F32)
    csum = jnp.dot(logf, upper, precision=lax.Precision.HIGHEST, preferred_element_type=F32)
    row = lax.broadcasted_iota(jnp.int32, v.shape, 0)
    return jnp.where(row < M_HEADS, v, csum)


def _gate_w_spec(layer):
    return pl.BlockSpec((1, GATE_COLS, D_MODEL), lambda *_: (layer, GATE_COL0 // GATE_COLS, 0))


def _norm1_body(x_ref, g_ref, sc_ref, sh_ref, gw_ref, gb_ref, o_ref, gate_ref):
    h = _norm_mod(x_ref[...], g_ref[...], sc_ref[...], sh_ref[...]).astype(o_ref.dtype)
    o_ref[...] = h
    gate_ref[...] = _gate_rows(h, gw_ref[0], gb_ref[...])


def _norm1(x, g, sc, sh, w_in_t, gate_bias, layer, chunk):
    s, d = x.shape
    vec = pl.BlockSpec((1, d), lambda i: (0, 0))
    return pl.pallas_call(
        _norm1_body,
        out_shape=(jax.ShapeDtypeStruct((s, d), BF16), jax.ShapeDtypeStruct((GATE_COLS, s), F32)),
        grid=(s // chunk,),
        in_specs=[pl.BlockSpec((chunk, d), lambda i: (i, 0)), vec, vec, vec, _gate_w_spec(layer),
                  pl.BlockSpec((GATE_COLS, 1), lambda i: (0, 0))],
        out_specs=(pl.BlockSpec((chunk, d), lambda i: (i, 0)), pl.BlockSpec((GATE_COLS, chunk), lambda i: (0, i))),
        compiler_params=_cparams(("arbitrary",)),
        name="norm1",
    )(x, g, sc, sh, w_in_t, gate_bias)


IN_PROJ_TN = 1024


def _in_proj_body(h_ref, wt_ref, p_ref, w_ref):
    @pl.when(pl.program_id(1) == 0)
    def _():
        w_ref[...] = wt_ref[0].astype(BF16)

    p_ref[...] = lax.dot_general(h_ref[...], w_ref[...], (((1,), (1,)), ((), ())),
                                 preferred_element_type=F32).astype(p_ref.dtype)


def _in_proj(h, w_in_t, layer):
    s, d = h.shape
    tm, tn = min(PROJ_TM, s), IN_PROJ_TN

    def w_rows(j, i):
        tiles = j * (tn // GATE_COLS) + jnp.where(j >= GATE_COL0 // tn, 1, 0)
        return (layer, pl.multiple_of(tiles * GATE_COLS, GATE_COLS), 0)

    return pl.pallas_call(
        _in_proj_body,
        out_shape=jax.ShapeDtypeStruct((s, P_W), BF16),
        grid=(P_W // tn, s // tm),
        in_specs=[pl.BlockSpec((tm, d), lambda j, i: (i, 0)),
                  pl.BlockSpec((pl.Element(1), pl.Element(tn), pl.Element(d)), w_rows)],
        out_specs=pl.BlockSpec((tm, tn), lambda j, i: (i, j)),
        scratch_shapes=[pltpu.VMEM((tn, d), BF16)],
        compiler_params=_cparams(("arbitrary", "arbitrary")),
        name="in_proj",
    )(h, w_in_t)


def _conv_silu(x, tail, w, b):
    row8 = lax.broadcasted_iota(jnp.int32, tail.shape, 0)
    y = b + w[CONV_K - 1:CONV_K, :] * x
    for d in range(1, CONV_K):
        rolled = pltpu.roll(x, d, 0)
        head = jnp.where(row8 < d, pltpu.roll(tail, d, 0), rolled[:SUBLANES])
        xd = jnp.concatenate([head, rolled[SUBLANES:]], axis=0)
        y = y + w[CONV_K - 1 - d:CONV_K - d, :] * xd
    return y * jax.nn.sigmoid(y)


def _mlstm_body(q_ref, k_ref, v_ref, cw_ref, cb_ref, gr_ref, gc_ref, o_ref, c_ref, m_ref, tq_ref, tk_ref):
    @pl.when(pl.program_id(0) == 0)
    def _():
        c_ref[...] = jnp.zeros_like(c_ref)
        m_ref[...] = jnp.zeros_like(m_ref)
        tq_ref[...] = jnp.zeros_like(tq_ref)
        tk_ref[...] = jnp.zeros_like(tk_ref)

    n = q_ref.shape[0]
    causal = (lax.broadcasted_iota(jnp.int32, (n, n), 1) <= lax.broadcasted_iota(jnp.int32, (n, n), 0))
    for hd in range(M_HEADS):
        qc = slice(hd * M_DQK, (hd + 1) * M_DQK)
        kc = slice(M_QK_W + hd * M_DQK, M_QK_W + (hd + 1) * M_DQK)
        vc = slice(hd * M_DV, (hd + 1) * M_DV)
        xq = q_ref[:, qc].astype(F32)
        xk = k_ref[:, qc].astype(F32)
        q = _conv_silu(xq, tq_ref[:, qc], cw_ref[:, qc], cb_ref[:, qc]) * (M_DQK ** -0.5)
        k = _conv_silu(xk, tk_ref[:, qc], cw_ref[:, kc], cb_ref[:, kc])
        tq_ref[:, qc] = xq[n - SUBLANES:, :]
        tk_ref[:, qc] = xk[n - SUBLANES:, :]
        qb = q.astype(BF16)
        kb = k.astype(BF16)
        vext = jnp.concatenate([v_ref[:, vc], jnp.ones((n, LANES), BF16)], axis=1)

        gr = gr_ref[hd]
        gc = gc_ref[hd]
        i_row, g_row = gr[0:1, :], gr[1:2, :]
        i_col, g_col = gc[:, 0:1], gc[:, 1:2]
        m = m_ref[hd]

        dmat = jnp.where(causal, g_col - g_row + i_row, NEG)
        inter = g_col + m
        m_row = jnp.maximum(inter, jnp.max(dmat, axis=-1, keepdims=True))
        w_intra = jnp.exp(dmat - m_row)
        a_inter = jnp.exp(inter - m_row)
        s_qk = lax.dot_general(qb, kb, (((1,), (1,)), ((), ())), preferred_element_type=F32) * w_intra
        state = c_ref[hd]
        tot = (jnp.dot(s_qk.astype(BF16), vext, preferred_element_type=F32)
               + a_inter * jnp.dot(qb, state.astype(BF16), preferred_element_type=F32))
        num = tot[:, :M_DV]
        den = tot[:, M_DV:M_DV + 1]
        h = num / jnp.maximum(jnp.abs(den), jnp.exp(-m_row))
        h = h * lax.rsqrt(jnp.mean(h * h, axis=-1, keepdims=True) + EPS)
        o_ref[:, vc] = h.astype(o_ref.dtype)

        g_last = g_col[n - 1:n, :]
        w_k = g_last - g_col + i_col
        m_new = jnp.maximum(g_last + m, jnp.max(w_k, axis=0, keepdims=True))
        decay = jnp.exp(g_last + m - m_new)
        kw = (k * jnp.exp(w_k - m_new)).astype(BF16)
        upd = lax.dot_general(kw, vext, (((0,), (0,)), ((), ())), preferred_element_type=F32)
        c_ref[hd] = decay * state + upd
        m_ref[hd] = m_new


def _mlstm(p, conv_w, conv_b, g_rows, g_cols, chunk):
    s = p.shape[0]
    return pl.pallas_call(
        _mlstm_body,
        out_shape=jax.ShapeDtypeStruct((s, M_V_W), BF16),
        grid=(s // chunk,),
        in_specs=[pl.BlockSpec((chunk, M_QK_W), lambda c: (c, P_MQ // M_QK_W)),
                  pl.BlockSpec((chunk, M_QK_W), lambda c: (c, P_MK // M_QK_W)),
                  pl.BlockSpec((chunk, M_V_W), lambda c: (c, P_MV // M_V_W)),
                  pl.BlockSpec((CONV_K, 2 * M_QK_W), lambda c: (0, 0)),
                  pl.BlockSpec((1, 2 * M_QK_W), lambda c: (0, 0)),
                  pl.BlockSpec((M_HEADS, 2, chunk), lambda c: (0, 0, c)),
                  pl.BlockSpec((M_HEADS, chunk, 2), lambda c: (0, c, 0))],
        out_specs=pl.BlockSpec((chunk, M_V_W), lambda c: (c, 0)),
        scratch_shapes=[pltpu.VMEM((M_HEADS, M_DQK, M_DV + LANES), F32), pltpu.VMEM((M_HEADS, 1, 1), F32),
                        pltpu.VMEM((SUBLANES, M_QK_W), F32), pltpu.VMEM((SUBLANES, M_QK_W), F32)],
        compiler_params=_cparams(("arbitrary",)),
        name="mlstm",
    )(p, p, p, conv_w, conv_b, g_rows, g_cols)


SB_UNDERFLOW = 105.0
SB_HEADS_PER_STEP = 4


def _sb_body(q_ref, k_ref, v_ref, o_ref, *, tile):
    i = pl.program_id(1)
    rows = lax.broadcasted_iota(jnp.int32, (tile, tile), 0)
    cols = lax.broadcasted_iota(jnp.int32, (tile, tile), 1)
    later = (rows > cols).astype(BF16)
    diag_mask = cols < rows

    def head(hh, with_prev):
        lanes = slice(hh * SB_DH, (hh + 1) * SB_DH)
        q = (q_ref[:, lanes].astype(F32) * (SB_DH ** -0.5)).astype(BF16)

        def logits(j, masked):
            kb = k_ref[pl.ds(pl.multiple_of(j * tile, tile), tile), lanes]
            z = lax.dot_general(q, kb, (((1,), (1,)), ((), ())), preferred_element_type=F32)
            sp = jnp.maximum(z, 0.0) + jnp.log(1.0 + jnp.exp(-jnp.abs(z)))
            if masked:
                sp = jnp.where(diag_mask, sp, 0.0)
            within = jnp.dot(sp.astype(BF16), later, preferred_element_type=F32)
            return z, sp, within, jnp.sum(sp, axis=-1, keepdims=True)

        def weighted(j, z, sp, within, rsum, masked):
            vb = v_ref[pl.ds(pl.multiple_of(j * tile, tile), tile), lanes]
            a = jnp.exp((z - sp) - within - rsum)
            if masked:
                a = jnp.where(diag_mask, a, 0.0)
            return jnp.dot(a.astype(BF16), vb, preferred_element_type=F32)

        zd, spd, wd, sd = logits(i, True)
        if not with_prev:
            return weighted(i, zd, spd, wd, jnp.zeros((tile, 1), F32), True), None, None
        zp, spp, wp, sprev = logits(i - 1, False)
        acc = (weighted(i, zd, spd, wd, jnp.zeros((tile, 1), F32), True)
               + weighted(i - 1, zp, spp, wp, sd, False))

        def cond(c):
            return jnp.logical_and(c[0] >= 0, jnp.min(c[1]) <= SB_UNDERFLOW)

        def body(c):
            j, rs, ac = c
            z, sp, w, sj = logits(j, False)
            return j - 1, rs + sj, ac + weighted(j, z, sp, w, rs, False)

        def scan_earlier(rs, ac):
            return lax.while_loop(cond, body, (i - 2, rs, ac))[2]

        return acc, sd + sprev, scan_earlier

    def store(hh, acc):
        o_ref[:, hh * SB_DH:(hh + 1) * SB_DH] = acc.astype(o_ref.dtype)

    @pl.when(i == 0)
    def _():
        for hh in range(SB_HEADS_PER_STEP):
            store(hh, head(hh, False)[0])

    @pl.when(i > 0)
    def _():
        firsts = [head(hh, True) for hh in range(SB_HEADS_PER_STEP)]
        for hh, (acc, rsum, scan_earlier) in enumerate(firsts):
            store(hh, scan_earlier(rsum, acc))


def _stick_breaking(p, tile):
    s = p.shape[0]
    w = SB_HEADS_PER_STEP * SB_DH
    cq, ck, cv = P_SQ // w, P_SK // w, P_SV // w
    return pl.pallas_call(
        functools.partial(_sb_body, tile=tile),
        out_shape=jax.ShapeDtypeStruct((s, SB_W), BF16),
        grid=(SB_HEADS // SB_HEADS_PER_STEP, s // tile),
        in_specs=[pl.BlockSpec((tile, w), lambda h, i: (i, cq + h)),
                  pl.BlockSpec((s, w), lambda h, i: (0, ck + h)),
                  pl.BlockSpec((s, w), lambda h, i: (0, cv + h))],
        out_specs=pl.BlockSpec((tile, w), lambda h, i: (i, h)),
        compiler_params=_cparams(("arbitrary", "arbitrary")),
        name="stick_breaking",
    )(p, p, p)


def _out_proj_body(mo_ref, gm_ref, gs_ref, hm_ref, hs_ref, ng_ref, x_ref, g1_ref, w_ref, o_ref):
    sig = jax.nn.sigmoid
    hm = sig(mo_ref[...].astype(F32)) * (hm_ref[...].astype(F32) * ng_ref[...])
    y = sig(gm_ref[...].astype(F32)) * hm + sig(gs_ref[...].astype(F32)) * hs_ref[...].astype(F32)
    o_ref[...] = x_ref[...] + g1_ref[...] * jnp.dot(y.astype(BF16), w_ref[...], preferred_element_type=F32)


def _out_proj(p, hm, hs, norm_g, x, g1, w_out):
    s, d = x.shape
    tm = min(ROW_TM, s)
    row = lambda blk: pl.BlockSpec((tm, d), lambda i: (i, blk))
    vec = pl.BlockSpec((1, d), lambda i: (0, 0))
    return pl.pallas_call(
        _out_proj_body,
        out_shape=jax.ShapeDtypeStruct((s, d), F32),
        grid=(s // tm,),
        in_specs=[row(P_MO // d), row(P_GM // d), row(P_GS // d), row(0), row(0), vec, row(0), vec,
                  pl.BlockSpec((d, d), lambda i: (0, 0), pipeline_mode=pl.Buffered(1))],
        out_specs=row(0),
        compiler_params=_cparams(("arbitrary",)),
        name="out_proj",
    )(p, p, p, hm, hs, norm_g, x, g1, w_out)


def _router_body(x_ref, g_ref, sc_ref, sh_ref, rwt_ref, rb_ref, h_ref, eid_ref, wt_ref):
    h = _norm_mod(x_ref[...], g_ref[...], sc_ref[...], sh_ref[...])
    h_ref[...] = h
    nt = (((1,), (1,)), ((), ()))
    h_hi = h.astype(BF16)
    h_lo = (h - h_hi.astype(F32)).astype(BF16)
    rw = rwt_ref[...]
    rw_hi = rw.astype(BF16)
    rw_lo = (rw - rw_hi.astype(F32)).astype(BF16)
    logits = (lax.dot_general(rw_hi, h_hi, nt, preferred_element_type=F32)
              + lax.dot_general(rw_hi, h_lo, nt, preferred_element_type=F32)
              + lax.dot_general(rw_lo, h_hi, nt, preferred_element_type=F32))
    scores = jax.nn.sigmoid(logits)
    sel = scores + rb_ref[...]
    t = sel.shape[1]
    idx = lax.broadcasted_iota(jnp.int32, (EXPERTS_PER_GROUP, t), 0)
    best = None
    for g in range(N_GROUPS):
        sl = slice(g * EXPERTS_PER_GROUP, (g + 1) * EXPERTS_PER_GROUP)
        s, sc = sel[sl], scores[sl]
        m1 = jnp.max(s, axis=0, keepdims=True)
        i1 = jnp.min(jnp.where(s == m1, idx, EXPERTS_PER_GROUP), axis=0, keepdims=True)
        s2 = jnp.where(idx == i1, -jnp.inf, s)
        m2 = jnp.max(s2, axis=0, keepdims=True)
        i2 = jnp.min(jnp.where(s2 == m2, idx, EXPERTS_PER_GROUP), axis=0, keepdims=True)
        w1 = jnp.sum(jnp.where(idx == i1, sc, 0.0), axis=0, keepdims=True)
        w2 = jnp.sum(jnp.where(idx == i2, sc, 0.0), axis=0, keepdims=True)
        cand = (m1 + m2, i1 + g * EXPERTS_PER_GROUP, i2 + g * EXPERTS_PER_GROUP, w1, w2)
        if best is None:
            best = cand
        else:
            better = cand[0] > best[0]
            best = tuple(jnp.where(better, c, b) for c, b in zip(cand, best))
    _, e1, e2, w1, w2 = best
    wsum = w1 + w2
    eid_ref[...] = jnp.concatenate([e1, e2], axis=0)
    wt_ref[...] = jnp.concatenate([w1 / wsum, w2 / wsum], axis=0)


def _router(x, g, sc, sh, router_wt, router_bias):
    s, d = x.shape
    tm = min(ROW_TM, s)
    vec = pl.BlockSpec((1, d), lambda i: (0, 0))
    return pl.pallas_call(
        _router_body,
        out_shape=(jax.ShapeDtypeStruct((s, d), F32), jax.ShapeDtypeStruct((TOP_K, s), jnp.int32),
                   jax.ShapeDtypeStruct((TOP_K, s), F32)),
        grid=(s // tm,),
        in_specs=[pl.BlockSpec((tm, d), lambda i: (i, 0)), vec, vec, vec,
                  pl.BlockSpec((N_EXPERTS, d), lambda i: (0, 0)),
                  pl.BlockSpec((N_EXPERTS, 1), lambda i: (0, 0))],
        out_specs=(pl.BlockSpec((tm, d), lambda i: (i, 0)), pl.BlockSpec((TOP_K, tm), lambda i: (0, i)),
                   pl.BlockSpec((TOP_K, tm), lambda i: (0, i))),
        compiler_params=_cparams(("arbitrary",)),
        name="router",
    )(x, g, sc, sh, router_wt, router_bias.reshape(N_EXPERTS, 1))


def _dispatch(eid, bm):
    s = eid.shape[1]
    n_assign = TOP_K * s
    flat_e = eid.reshape(-1)
    onehot = (flat_e[:, None] == jnp.arange(N_EXPERTS, dtype=jnp.int32)[None, :]).astype(jnp.int32)
    csum = jnp.cumsum(onehot, axis=0)
    rank = jnp.sum(csum * onehot, axis=1) - 1
    counts = csum[-1]
    padded = (counts + bm - 1) // bm * bm
    pends = jnp.cumsum(padded)
    dest = (pends - padded)[flat_e] + rank
    nb_max = pl.cdiv(n_assign, bm) + N_EXPERTS
    flat_t = jnp.tile(jnp.arange(s, dtype=jnp.int32), TOP_K)
    slot_tok = (jnp.arange(nb_max * bm, dtype=jnp.int32) % s).at[dest].set(flat_t)
    nb_used = (pends[-1] // bm).astype(jnp.int32).reshape(1)
    block_start = jnp.arange(nb_max, dtype=jnp.int32) * bm
    block_e = jnp.minimum(jnp.sum((pends[None, :] <= block_start[:, None]).astype(jnp.int32), axis=1),
                          N_EXPERTS - 1)
    return dest.astype(jnp.int32), slot_tok, block_e, nb_used, nb_max


def _expert_schedule(block_e, nb_used):
    nb_max = block_e.shape[0]
    idx = jnp.arange(nb_max, dtype=jnp.int32)
    used = idx < nb_used[0]
    prev_e = jnp.concatenate([jnp.full((1,), -1, jnp.int32), block_e[:-1]])
    first = (used & (block_e != prev_e)).astype(jnp.int32)
    pos = jnp.where(first == 1, idx, nb_max)
    after = jnp.concatenate([lax.cummin(pos[::-1])[::-1][1:], jnp.full((1,), nb_max, jnp.int32)])
    nxt = jnp.where(after < nb_max, block_e[jnp.minimum(after, nb_max - 1)], -1)
    return first, nxt.astype(jnp.int32)


def _expert_body(be_ref, first_ref, nxt_ref, st_ref, nu_ref, h_hbm, wg_hbm, wu_hbm, wd_hbm, y_ref,
                 xa_ref, xb_ref, wg_st, wu_st, wd_st, wg_bf, wu_bf, wd_bf, gsem, wsem, *, bm, layer):
    b = pl.program_id(0)
    nb = nu_ref[0]
    n_chunks = D_FF_EXPERT // MOE_FF_CHUNK

    def row_copy(tok, r, buf, si):
        return pltpu.make_async_copy(h_hbm.at[pl.ds(tok, 1), :], buf.at[pl.ds(r, 1), :], gsem.at[si])

    def wait_gather(buf, si):
        pltpu.make_async_copy(h_hbm.at[pl.ds(0, bm), :], buf, gsem.at[si]).wait()

    def weight_copies(e):
        return (pltpu.make_async_copy(wg_hbm.at[layer, e], wg_st, wsem.at[0]),
                pltpu.make_async_copy(wu_hbm.at[layer, e], wu_st, wsem.at[1]),
                pltpu.make_async_copy(wd_hbm.at[layer, e], wd_st, wsem.at[2]))

    @pl.when(b == 0)
    def _():
        for cp in weight_copies(be_ref[0]):
            cp.start()

        def go(r, c):
            row_copy(st_ref[r], r, xa_ref, 0).start()
            return c
        lax.fori_loop(0, bm, go, 0, unroll=MOE_GATHER_UNROLL)

    @pl.when((b < nb) & (first_ref[b] == 1))
    def _():
        for cp in weight_copies(be_ref[b]):
            cp.wait()
        for c in range(n_chunks):
            cs = slice(c * MOE_FF_CHUNK, (c + 1) * MOE_FF_CHUNK)
            wg_bf[:, cs] = wg_st[:, cs].astype(BF16)
            wu_bf[:, cs] = wu_st[:, cs].astype(BF16)
            wd_bf[cs, :] = wd_st[cs, :].astype(BF16)

        @pl.when(nxt_ref[b] >= 0)
        def _():
            for cp in weight_copies(nxt_ref[b]):
                cp.start()

    def step(cur, cur_si, nxt_buf, nxt_si):
        wait_gather(cur, cur_si)
        base = jnp.minimum(b + 1, nb - 1) * bm
        x = cur[...].astype(BF16)
        y = None
        for c in range(n_chunks):
            for r in range(c * bm // n_chunks, (c + 1) * bm // n_chunks):
                row_copy(st_ref[base + r], r, nxt_buf, nxt_si).start(priority=r % 2)
            cs = slice(c * MOE_FF_CHUNK, (c + 1) * MOE_FF_CHUNK)
            gate = jnp.dot(x, wg_bf[:, cs], preferred_element_type=F32)
            up = jnp.dot(x, wu_bf[:, cs], preferred_element_type=F32)
            mid = (gate * jax.nn.sigmoid(gate) * up).astype(BF16)
            part = jnp.dot(mid, wd_bf[cs, :], preferred_element_type=F32)
            y = part if y is None else y + part
        y_ref[...] = y

        @pl.when(b == nb - 1)
        def _():
            wait_gather(nxt_buf, nxt_si)

    @pl.when((b < nb) & (b % 2 == 0))
    def _():
        step(xa_ref, 0, xb_ref, 1)

    @pl.when((b < nb) & (b % 2 == 1))
    def _():
        step(xb_ref, 1, xa_ref, 0)

    @pl.when(b >= nb)
    def _():
        y_ref[...] = jnp.zeros_like(y_ref)


def _experts(h, slot_tok, block_e, nb_used, nb_max, layer, w_gate, w_up, w_down, bm):
    d = h.shape[1]
    ff = D_FF_EXPERT
    first, nxt = _expert_schedule(block_e, nb_used)
    hbm = pl.BlockSpec(memory_space=pl.ANY)
    grid_spec = pltpu.PrefetchScalarGridSpec(
        num_scalar_prefetch=5,
        grid=(nb_max,),
        in_specs=[hbm, hbm, hbm, hbm],
        out_specs=pl.BlockSpec((bm, d), lambda b, *_: (b, 0)),
        scratch_shapes=[pltpu.VMEM((bm, d), F32), pltpu.VMEM((bm, d), F32),
                        pltpu.VMEM((d, ff), F32), pltpu.VMEM((d, ff), F32), pltpu.VMEM((ff, d), F32),
                        pltpu.VMEM((d, ff), BF16), pltpu.VMEM((d, ff), BF16), pltpu.VMEM((ff, d), BF16),
                        pltpu.SemaphoreType.DMA((2,)), pltpu.SemaphoreType.DMA((3,))],
    )
    return pl.pallas_call(
        functools.partial(_expert_body, bm=bm, layer=layer),
        out_shape=jax.ShapeDtypeStruct((nb_max * bm, d), F32),
        grid_spec=grid_spec,
        compiler_params=pltpu.CompilerParams(dimension_semantics=("arbitrary",),
                                             vmem_limit_bytes=VMEM_LIMIT),
        name="experts",
    )(block_e, first, nxt, slot_tok, nb_used, h, w_gate, w_up, w_down)


def _combine_body(pos_ref, y_hbm, w_ref, x_ref, g2_ref, ng_ref, *rest, tc, s, final):
    if final:
        o_ref, yb_ref, sem = rest
    else:
        nsc_ref, nsh_ref, gw_ref, gb_ref, o_ref, h_ref, gate_ref, yb_ref, sem = rest
    i = pl.program_id(0)
    n = pl.num_programs(0)

    def start_gather(blk, slot):
        for k in range(TOP_K):
            for r in range(tc):
                src = pos_ref[k * s + blk * tc + r]
                pltpu.make_async_copy(y_hbm.at[pl.ds(src, 1), :], yb_ref.at[slot, k, pl.ds(r, 1), :],
                                      sem.at[slot]).start(priority=k)

    def wait_gather(slot):
        for k in range(TOP_K):
            pltpu.make_async_copy(y_hbm.at[pl.ds(0, tc), :], yb_ref.at[slot, k], sem.at[slot]).wait()

    slot = i % 2

    @pl.when(i == 0)
    def _():
        start_gather(0, 0)

    @pl.when(i + 1 < n)
    def _():
        start_gather(i + 1, 1 - slot)

    wait_gather(slot)
    w = w_ref[...]
    moe = w[:, 0:1] * yb_ref[slot, 0] + w[:, 1:2] * yb_ref[slot, 1]
    xn = x_ref[...] + g2_ref[...] * moe
    if final:
        o_ref[...] = (xn * lax.rsqrt(jnp.mean(xn * xn, axis=-1, keepdims=True) + EPS)) * ng_ref[...]
    else:
        o_ref[...] = xn
        h = _norm_mod(xn, ng_ref[...], nsc_ref[...], nsh_ref[...]).astype(h_ref.dtype)
        h_ref[...] = h
        gate_ref[...] = _gate_rows(h, gw_ref[0], gb_ref[...])


def _combine(ys, pos, wts, x, g2, tc, norm_g, next_layer=None):
    s, d = x.shape
    final = next_layer is None
    row = pl.BlockSpec((tc, d), lambda i, pos: (i, 0))
    vec = pl.BlockSpec((1, d), lambda i, pos: (0, 0))
    if final:
        extra_args, extra_specs = (), []
    else:
        norm_sc, norm_sh, w_in_t, gate_bias, layer = next_layer
        extra_args = (norm_sc, norm_sh, w_in_t, gate_bias)
        extra_specs = [vec, vec, _gate_w_spec(layer), pl.BlockSpec((GATE_COLS, 1), lambda i, pos: (0, 0))]
    grid_spec = pltpu.PrefetchScalarGridSpec(
        num_scalar_prefetch=1,
        grid=(s // tc,),
        in_specs=[pl.BlockSpec(memory_space=pl.ANY), pl.BlockSpec((tc, TOP_K), lambda i, pos: (i, 0)), row, vec, vec]
        + extra_specs,
        out_specs=row if final else (row, row, pl.BlockSpec((GATE_COLS, tc), lambda i, pos: (0, i))),
        scratch_shapes=[pltpu.VMEM((2, TOP_K, tc, d), F32), pltpu.SemaphoreType.DMA((2,))],
    )
    out_f32 = jax.ShapeDtypeStruct((s, d), F32)
    return pl.pallas_call(
        functools.partial(_combine_body, tc=tc, s=s, final=final),
        out_shape=out_f32 if final else (out_f32, jax.ShapeDtypeStruct((s, d), BF16),
                                         jax.ShapeDtypeStruct((GATE_COLS, s), F32)),
        grid_spec=grid_spec,
        compiler_params=_cparams(("arbitrary",)),
        name="moe_combine",
    )(pos, ys, wts, x, g2, norm_g, *extra_args)


def kernel(x, c, ada_w, ada_b, norm1_g, w_in, m_conv_w, m_conv_b, m_igate_b, m_fgate_b, m_norm_g, w_out,
           norm2_g, router_w, router_bias, w_gate, w_up, w_down, final_g):
    batch, s, d = x.shape
    assert batch == 1 and d == D_MODEL
    assert MLSTM_CHUNK == COMBINE_TC
    chunk = min(MLSTM_CHUNK, s)
    sb_tile = min(SB_TILE, s)
    tc = chunk
    xs = x.reshape(s, d)
    mod = _ada(c, ada_w, ada_b)
    router_wt = router_w.T
    w_in_t = jnp.swapaxes(w_in, 1, 2)
    mods = [[mod[l, :, i * d:(i + 1) * d] for i in range(6)] for l in range(DEPTH)]
    gate_bias = [jnp.concatenate([m_igate_b[l], m_fgate_b[l]]).reshape(GATE_COLS, 1) for l in range(DEPTH)]
    h1, gproc = _norm1(xs, norm1_g[0].reshape(1, d), mods[0][1], mods[0][0], w_in_t, gate_bias[0], 0, chunk)
    for l in range(DEPTH):
        sh1, sc1, g1, sh2, sc2, g2 = mods[l]
        p = _in_proj(h1, w_in_t, l)
        g_rows = gproc.reshape(2, M_HEADS, s).transpose(1, 0, 2)
        g_cols = g_rows.transpose(0, 2, 1)
        hm = _mlstm(p, m_conv_w[l], m_conv_b[l].reshape(1, -1), g_rows, g_cols, chunk)
        hs = _stick_breaking(p, sb_tile)
        xs = _out_proj(p, hm, hs, m_norm_g[l].reshape(1, d), xs, g1, w_out[l].astype(BF16))
        h2, eid, wts = _router(xs, norm2_g[l].reshape(1, d), sc2, sh2, router_wt, router_bias)
        dest, slot_tok, block_e, nb_used, nb_max = _dispatch(eid, MOE_BM)
        ys = _experts(h2, slot_tok, block_e, nb_used, nb_max, l, w_gate, w_up, w_down, MOE_BM)
        if l + 1 < DEPTH:
            xs, h1, gproc = _combine(ys, dest, wts.T, xs, g2, tc, norm1_g[l + 1].reshape(1, d),
                                     (mods[l + 1][1], mods[l + 1][0], w_in_t, gate_bias[l + 1], l + 1))
        else:
            out = _combine(ys, dest, wts.T, xs, g2, tc, final_g.reshape(1, d))
    return out.reshape(batch, s, d)
```

```python
import functools

import jax
import jax.numpy as jnp
from jax import lax
from jax.experimental import pallas as pl
from jax.experimental.pallas import tpu as pltpu

F32 = jnp.float32
BF16 = jnp.bfloat16

D_MODEL = 2048
DEPTH = 2
M_HEADS = 4
M_DV = D_MODEL // M_HEADS
M_DQK = M_DV // 2
CONV_K = 4
SB_HEADS = 16
SB_DH = D_MODEL // SB_HEADS
N_EXPERTS = 32
N_GROUPS = 4
EXPERTS_PER_GROUP = N_EXPERTS // N_GROUPS
TOP_K = 2
D_FF_EXPERT = 768
EPS = 1e-6
NEG = -1e30

M_QK_W = M_HEADS * M_DQK
M_V_W = M_HEADS * M_DV
SB_W = SB_HEADS * SB_DH
GATE_COL0 = 2 * M_QK_W + 2 * M_V_W
GATE_COLS = 2 * M_HEADS
P_MQ, P_MK, P_MV, P_MO = 0, M_QK_W, 2 * M_QK_W, 2 * M_QK_W + M_V_W
P_SQ = GATE_COL0
P_SK, P_SV, P_GM, P_GS = P_SQ + SB_W, P_SQ + 2 * SB_W, P_SQ + 3 * SB_W, P_SQ + 4 * SB_W
P_W = P_SQ + 5 * SB_W

LANES = 128
SUBLANES = 8
VMEM_LIMIT = 56 * 1024 * 1024

MLSTM_CHUNK = 256
SB_TILE = 256
MOE_BM = 512
MOE_FF_CHUNK = 256
MOE_GATHER_UNROLL = 8
COMBINE_TC = 256
ADA_TN = 1024
PROJ_TM = 2048
ROW_TM = 512


def _cparams(sem):
    return pltpu.CompilerParams(dimension_semantics=sem, vmem_limit_bytes=VMEM_LIMIT)


def _ada_body(c_ref, w_ref, b_ref, o_ref):
    c = c_ref[...]
    cs = c * jax.nn.sigmoid(c)
    for j in range(w_ref.shape[2] // LANES):
        sl = slice(j * LANES, (j + 1) * LANES)
        o_ref[0, :, sl] = jnp.sum(w_ref[0, :, sl] * cs, axis=0, keepdims=True) + b_ref[0, :, sl]


def _ada(c, ada_w, ada_b):
    depth, d, n = ada_w.shape
    tn = ADA_TN
    cb = jnp.broadcast_to(c.reshape(d, 1), (d, LANES))
    return pl.pallas_call(
        _ada_body,
        out_shape=jax.ShapeDtypeStruct((depth, 1, n), F32),
        grid=(depth, n // tn),
        in_specs=[pl.BlockSpec((d, LANES), lambda l, j: (0, 0)),
                  pl.BlockSpec((1, d, tn), lambda l, j: (l, 0, j)),
                  pl.BlockSpec((1, 1, tn), lambda l, j: (l, 0, j))],
        out_specs=pl.BlockSpec((1, 1, tn), lambda l, j: (l, 0, j)),
        compiler_params=_cparams(("arbitrary", "arbitrary")),
        name="ada_mod",
    )(cb, ada_w, ada_b.reshape(depth, 1, n))


def _norm_mod(x, g, sc, sh):
    y = x * lax.rsqrt(jnp.mean(x * x, axis=-1, keepdims=True) + EPS)
    return (y * g) * (1.0 + sc) + sh


def _gate_rows(h, w, b):
    v = lax.dot_general(w.astype(BF16), h, (((1,), (1,)), ((), ())), preferred_element_type=F32) + b
    n = v.shape[1]
    logf = jnp.minimum(v, 0.0) - jnp.log(1.0 + jnp.exp(-jnp.abs(v)))
    upper = (lax.broadcasted_iota(jnp.int32, (n, n), 0)
             <= lax.broadcasted_iota(jnp.int32, (n, n), 1)).astype(F32)
    csum = jnp.dot(logf, upper, precision=lax.Precision.HIGHEST, preferred_element_type=F32)
    row = lax.broadcasted_iota(jnp.int32, v.shape, 0)
    return jnp.where(row < M_HEADS, v, csum)


def _gate_w_spec(layer):
    return pl.BlockSpec((1, GATE_COLS, D_MODEL), lambda *_: (layer, GATE_COL0 // GATE_COLS, 0))


def _norm1_body(x_ref, g_ref, sc_ref, sh_ref, gw_ref, gb_ref, o_ref, gate_ref):
    h = _norm_mod(x_ref[...], g_ref[...], sc_ref[...], sh_ref[...]).astype(o_ref.dtype)
    o_ref[...] = h
    gate_ref[...] = _gate_rows(h, gw_ref[0], gb_ref[...])


def _norm1(x, g, sc, sh, w_in_t, gate_bias, layer, chunk):
    s, d = x.shape
    vec = pl.BlockSpec((1, d), lambda i: (0, 0))
    return pl.pallas_call(
        _norm1_body,
        out_shape=(jax.ShapeDtypeStruct((s, d), BF16), jax.ShapeDtypeStruct((GATE_COLS, s), F32)),
        grid=(s // chunk,),
        in_specs=[pl.BlockSpec((chunk, d), lambda i: (i, 0)), vec, vec, vec, _gate_w_spec(layer),
                  pl.BlockSpec((GATE_COLS, 1), lambda i: (0, 0))],
        out_specs=(pl.BlockSpec((chunk, d), lambda i: (i, 0)), pl.BlockSpec((GATE_COLS, chunk), lambda i: (0, i))),
        compiler_params=_cparams(("arbitrary",)),
        name="norm1",
    )(x, g, sc, sh, w_in_t, gate_bias)


IN_PROJ_TN = 1024


def _in_proj_body(h_ref, wt_ref, p_ref, w_ref):
    @pl.when(pl.program_id(1) == 0)
    def _():
        w_ref[...] = wt_ref[0].astype(BF16)

    p_ref[...] = lax.dot_general(h_ref[...], w_ref[...], (((1,), (1,)), ((), ())),
                                 preferred_element_type=F32).astype(p_ref.dtype)


def _in_proj(h, w_in_t, layer):
    s, d = h.shape
    tm, tn = min(PROJ_TM, s), IN_PROJ_TN

    def w_rows(j, i):
        tiles = j * (tn // GATE_COLS) + jnp.where(j >= GATE_COL0 // tn, 1, 0)
        return (layer, pl.multiple_of(tiles * GATE_COLS, GATE_COLS), 0)

    return pl.pallas_call(
        _in_proj_body,
        out_shape=jax.ShapeDtypeStruct((s, P_W), BF16),
        grid=(P_W // tn, s // tm),
        in_specs=[pl.BlockSpec((tm, d), lambda j, i: (i, 0)),
                  pl.BlockSpec((pl.Element(1), pl.Element(tn), pl.Element(d)), w_rows)],
        out_specs=pl.BlockSpec((tm, tn), lambda j, i: (i, j)),
        scratch_shapes=[pltpu.VMEM((tn, d), BF16)],
        compiler_params=_cparams(("arbitrary", "arbitrary")),
        name="in_proj",
    )(h, w_in_t)


def _conv_silu(x, tail, w, b):
    row8 = lax.broadcasted_iota(jnp.int32, tail.shape, 0)
    y = b + w[CONV_K - 1:CONV_K, :] * x
    for d in range(1, CONV_K):
        rolled = pltpu.roll(x, d, 0)
        head = jnp.where(row8 < d, pltpu.roll(tail, d, 0), rolled[:SUBLANES])
        xd = jnp.concatenate([head, rolled[SUBLANES:]], axis=0)
        y = y + w[CONV_K - 1 - d:CONV_K - d, :] * xd
    return y * jax.nn.sigmoid(y)


def _mlstm_body(q_ref, k_ref, v_ref, cw_ref, cb_ref, gr_ref, gc_ref, o_ref, c_ref, m_ref, tq_ref, tk_ref):
    @pl.when(pl.program_id(0) == 0)
    def _():
        c_ref[...] = jnp.zeros_like(c_ref)
        m_ref[...] = jnp.zeros_like(m_ref)
        tq_ref[...] = jnp.zeros_like(tq_ref)
        tk_ref[...] = jnp.zeros_like(tk_ref)

    n = q_ref.shape[0]
    causal = (lax.broadcasted_iota(jnp.int32, (n, n), 1) <= lax.broadcasted_iota(jnp.int32, (n, n), 0))
    for hd in range(M_HEADS):
        qc = slice(hd * M_DQK, (hd + 1) * M_DQK)
        kc = slice(M_QK_W + hd * M_DQK, M_QK_W + (hd + 1) * M_DQK)
        vc = slice(hd * M_DV, (hd + 1) * M_DV)
        xq = q_ref[:, qc].astype(F32)
        xk = k_ref[:, qc].astype(F32)
        q = _conv_silu(xq, tq_ref[:, qc], cw_ref[:, qc], cb_ref[:, qc]) * (M_DQK ** -0.5)
        k = _conv_silu(xk, tk_ref[:, qc], cw_ref[:, kc], cb_ref[:, kc])
        tq_ref[:, qc] = xq[n - SUBLANES:, :]
        tk_ref[:, qc] = xk[n - SUBLANES:, :]
        qb = q.astype(BF16)
        kb = k.astype(BF16)
        vext = jnp.concatenate([v_ref[:, vc], jnp.ones((n, LANES), BF16)], axis=1)

        gr = gr_ref[hd]
        gc = gc_ref[hd]
        i_row, g_row = gr[0:1, :], gr[1:2, :]
        i_col, g_col = gc[:, 0:1], gc[:, 1:2]
        m = m_ref[hd]

        dmat = jnp.where(causal, g_col - g_row + i_row, NEG)
        inter = g_col + m
        m_row = jnp.maximum(inter, jnp.max(dmat, axis=-1, keepdims=True))
        w_intra = jnp.exp(dmat - m_row)
        a_inter = jnp.exp(inter - m_row)
        s_qk = lax.dot_general(qb, kb, (((1,), (1,)), ((), ())), preferred_element_type=F32) * w_intra
        state = c_ref[hd]
        tot = (jnp.dot(s_qk.astype(BF16), vext, preferred_element_type=F32)
               + a_inter * jnp.dot(qb, state.astype(BF16), preferred_element_type=F32))
        num = tot[:, :M_DV]
        den = tot[:, M_DV:M_DV + 1]
        h = num / jnp.maximum(jnp.abs(den), jnp.exp(-m_row))
        h = h * lax.rsqrt(jnp.mean(h * h, axis=-1, keepdims=True) + EPS)
        o_ref[:, vc] = h.astype(o_ref.dtype)

        g_last = g_col[n - 1:n, :]
        w_k = g_last - g_col + i_col
        m_new = jnp.maximum(g_last + m, jnp.max(w_k, axis=0, keepdims=True))
        decay = jnp.exp(g_last + m - m_new)
        kw = (k * jnp.exp(w_k - m_new)).astype(BF16)
        upd = lax.dot_general(kw, vext, (((0,), (0,)), ((), ())), preferred_element_type=F32)
        c_ref[hd] = decay * state + upd
        m_ref[hd] = m_new


def _mlstm(p, conv_w, conv_b, g_rows, g_cols, chunk):
    s = p.shape[0]
    return pl.pallas_call(
        _mlstm_body,
        out_shape=jax.ShapeDtypeStruct((s, M_V_W), BF16),
        grid=(s // chunk,),
        in_specs=[pl.BlockSpec((chunk, M_QK_W), lambda c: (c, P_MQ // M_QK_W)),
                  pl.BlockSpec((chunk, M_QK_W), lambda c: (c, P_MK // M_QK_W)),
                  pl.BlockSpec((chunk, M_V_W), lambda c: (c, P_MV // M_V_W)),
                  pl.BlockSpec((CONV_K, 2 * M_QK_W), lambda c: (0, 0)),
                  pl.BlockSpec((1, 2 * M_QK_W), lambda c: (0, 0)),
                  pl.BlockSpec((M_HEADS, 2, chunk), lambda c: (0, 0, c)),
                  pl.BlockSpec((M_HEADS, chunk, 2), lambda c: (0, c, 0))],
        out_specs=pl.BlockSpec((chunk, M_V_W), lambda c: (c, 0)),
        scratch_shapes=[pltpu.VMEM((M_HEADS, M_DQK, M_DV + LANES), F32), pltpu.VMEM((M_HEADS, 1, 1), F32),
                        pltpu.VMEM((SUBLANES, M_QK_W), F32), pltpu.VMEM((SUBLANES, M_QK_W), F32)],
        compiler_params=_cparams(("arbitrary",)),
        name="mlstm",
    )(p, p, p, conv_w, conv_b, g_rows, g_cols)


SB_UNDERFLOW = 105.0
SB_HEADS_PER_STEP = 4


def _sb_body(q_ref, k_ref, v_ref, o_ref, *, tile):
    i = pl.program_id(1)
    rows = lax.broadcasted_iota(jnp.int32, (tile, tile), 0)
    cols = lax.broadcasted_iota(jnp.int32, (tile, tile), 1)
    later = (rows > cols).astype(BF16)
    diag_mask = cols < rows

    def head(hh, with_prev):
        lanes = slice(hh * SB_DH, (hh + 1) * SB_DH)
        q = (q_ref[:, lanes].astype(F32) * (SB_DH ** -0.5)).astype(BF16)

        def logits(j, masked):
            kb = k_ref[pl.ds(pl.multiple_of(j * tile, tile), tile), lanes]
            z = lax.dot_general(q, kb, (((1,), (1,)), ((), ())), preferred_element_type=F32)
            sp = jnp.maximum(z, 0.0) + jnp.log(1.0 + jnp.exp(-jnp.abs(z)))
            if masked:
                sp = jnp.where(diag_mask, sp, 0.0)
            within = jnp.dot(sp.astype(BF16), later, preferred_element_type=F32)
            return z, sp, within, jnp.sum(sp, axis=-1, keepdims=True)

        def weighted(j, z, sp, within, rsum, masked):
            vb = v_ref[pl.ds(pl.multiple_of(j * tile, tile), tile), lanes]
            a = jnp.exp((z - sp) - within - rsum)
            if masked:
                a = jnp.where(diag_mask, a, 0.0)
            return jnp.dot(a.astype(BF16), vb, preferred_element_type=F32)

        zd, spd, wd, sd = logits(i, True)
        if not with_prev:
            return weighted(i, zd, spd, wd, jnp.zeros((tile, 1), F32), True), None, None
        zp, spp, wp, sprev = logits(i - 1, False)
        acc = (weighted(i, zd, spd, wd, jnp.zeros((tile, 1), F32), True)
               + weighted(i - 1, zp, spp, wp, sd, False))

        def cond(c):
            return jnp.logical_and(c[0] >= 0, jnp.min(c[1]) <= SB_UNDERFLOW)

        def body(c):
            j, rs, ac = c
            z, sp, w, sj = logits(j, False)
            return j - 1, rs + sj, ac + weighted(j, z, sp, w, rs, False)

        def scan_earlier(rs, ac):
            return lax.while_loop(cond, body, (i - 2, rs, ac))[2]

        return acc, sd + sprev, scan_earlier

    def store(hh, acc):
        o_ref[:, hh * SB_DH:(hh + 1) * SB_DH] = acc.astype(o_ref.dtype)

    @pl.when(i == 0)
    def _():
        for hh in range(SB_HEADS_PER_STEP):
            store(hh, head(hh, False)[0])

    @pl.when(i > 0)
    def _():
        firsts = [head(hh, True) for hh in range(SB_HEADS_PER_STEP)]
        for hh, (acc, rsum, scan_earlier) in enumerate(firsts):
            store(hh, scan_earlier(rsum, acc))


def _stick_breaking(p, tile):
    s = p.shape[0]
    w = SB_HEADS_PER_STEP * SB_DH
    cq, ck, cv = P_SQ // w, P_SK // w, P_SV // w
    return pl.pallas_call(
        functools.partial(_sb_body, tile=tile),
        out_shape=jax.ShapeDtypeStruct((s, SB_W), BF16),
        grid=(SB_HEADS // SB_HEADS_PER_STEP, s // tile),
        in_specs=[pl.BlockSpec((tile, w), lambda h, i: (i, cq + h)),
                  pl.BlockSpec((s, w), lambda h, i: (0, ck + h)),
                  pl.BlockSpec((s, w), lambda h, i: (0, cv + h))],
        out_specs=pl.BlockSpec((tile, w), lambda h, i: (i, h)),
        compiler_params=_cparams(("arbitrary", "arbitrary")),
        name="stick_breaking",
    )(p, p, p)


def _out_proj_body(mo_ref, gm_ref, gs_ref, hm_ref, hs_ref, ng_ref, x_ref, g1_ref, w_ref, o_ref):
    sig = jax.nn.sigmoid
    hm = sig(mo_ref[...].astype(F32)) * (hm_ref[...].astype(F32) * ng_ref[...])
    y = sig(gm_ref[...].astype(F32)) * hm + sig(gs_ref[...].astype(F32)) * hs_ref[...].astype(F32)
    o_ref[...] = x_ref[...] + g1_ref[...] * jnp.dot(y.astype(BF16), w_ref[...], preferred_element_type=F32)


def _out_proj(p, hm, hs, norm_g, x, g1, w_out):
    s, d = x.shape
    tm = min(ROW_TM, s)
    row = lambda blk: pl.BlockSpec((tm, d), lambda i: (i, blk))
    vec = pl.BlockSpec((1, d), lambda i: (0, 0))
    return pl.pallas_call(
        _out_proj_body,
        out_shape=jax.ShapeDtypeStruct((s, d), F32),
        grid=(s // tm,),
        in_specs=[row(P_MO // d), row(P_GM // d), row(P_GS // d), row(0), row(0), vec, row(0), vec,
                  pl.BlockSpec((d, d), lambda i: (0, 0), pipeline_mode=pl.Buffered(1))],
        out_specs=row(0),
        compiler_params=_cparams(("arbitrary",)),
        name="out_proj",
    )(p, p, p, hm, hs, norm_g, x, g1, w_out)


def _router_body(x_ref, g_ref, sc_ref, sh_ref, rwt_ref, rb_ref, h_ref, eid_ref, wt_ref):
    h = _norm_mod(x_ref[...], g_ref[...], sc_ref[...], sh_ref[...])
    h_ref[...] = h
    nt = (((1,), (1,)), ((), ()))
    h_hi = h.astype(BF16)
    h_lo = (h - h_hi.astype(F32)).astype(BF16)
    rw = rwt_ref[...]
    rw_hi = rw.astype(BF16)
    rw_lo = (rw - rw_hi.astype(F32)).astype(BF16)
    logits = (lax.dot_general(rw_hi, h_hi, nt, preferred_element_type=F32)
              + lax.dot_general(rw_hi, h_lo, nt, preferred_element_type=F32)
              + lax.dot_general(rw_lo, h_hi, nt, preferred_element_type=F32))
    scores = jax.nn.sigmoid(logits)
    sel = scores + rb_ref[...]
    t = sel.shape[1]
    idx = lax.broadcasted_iota(jnp.int32, (EXPERTS_PER_GROUP, t), 0)
    best = None
    for g in range(N_GROUPS):
        sl = slice(g * EXPERTS_PER_GROUP, (g + 1) * EXPERTS_PER_GROUP)
        s, sc = sel[sl], scores[sl]
        m1 = jnp.max(s, axis=0, keepdims=True)
        i1 = jnp.min(jnp.where(s == m1, idx, EXPERTS_PER_GROUP), axis=0, keepdims=True)
        s2 = jnp.where(idx == i1, -jnp.inf, s)
        m2 = jnp.max(s2, axis=0, keepdims=True)
        i2 = jnp.min(jnp.where(s2 == m2, idx, EXPERTS_PER_GROUP), axis=0, keepdims=True)
        w1 = jnp.sum(jnp.where(idx == i1, sc, 0.0), axis=0, keepdims=True)
        w2 = jnp.sum(jnp.where(idx == i2, sc, 0.0), axis=0, keepdims=True)
        cand = (m1 + m2, i1 + g * EXPERTS_PER_GROUP, i2 + g * EXPERTS_PER_GROUP, w1, w2)
        if best is None:
            best = cand
        else:
            better = cand[0] > best[0]
            best = tuple(jnp.where(better, c, b) for c, b in zip(cand, best))
    _, e1, e2, w1, w2 = best
    wsum = w1 + w2
    eid_ref[...] = jnp.concatenate([e1, e2], axis=0)
    wt_ref[...] = jnp.concatenate([w1 / wsum, w2 / wsum], axis=0)


def _router(x, g, sc, sh, router_wt, router_bias):
    s, d = x.shape
    tm = min(ROW_TM, s)
    vec = pl.BlockSpec((1, d), lambda i: (0, 0))
    return pl.pallas_call(
        _router_body,
        out_shape=(jax.ShapeDtypeStruct((s, d), F32), jax.ShapeDtypeStruct((TOP_K, s), jnp.int32),
                   jax.ShapeDtypeStruct((TOP_K, s), F32)),
        grid=(s // tm,),
        in_specs=[pl.BlockSpec((tm, d), lambda i: (i, 0)), vec, vec, vec,
                  pl.BlockSpec((N_EXPERTS, d), lambda i: (0, 0)),
                  pl.BlockSpec((N_EXPERTS, 1), lambda i: (0, 0))],
        out_specs=(pl.BlockSpec((tm, d), lambda i: (i, 0)), pl.BlockSpec((TOP_K, tm), lambda i: (0, i)),
                   pl.BlockSpec((TOP_K, tm), lambda i: (0, i))),
        compiler_params=_cparams(("arbitrary",)),
        name="router",
    )(x, g, sc, sh, router_wt, router_bias.reshape(N_EXPERTS, 1))


def _dispatch(eid, bm):
    s = eid.shape[1]
    n_assign = TOP_K * s
    flat_e = eid.reshape(-1)
    onehot = (flat_e[:, None] == jnp.arange(N_EXPERTS, dtype=jnp.int32)[None, :]).astype(jnp.int32)
    csum = jnp.cumsum(onehot, axis=0)
    rank = jnp.sum(csum * onehot, axis=1) - 1
    counts = csum[-1]
    padded = (counts + bm - 1) // bm * bm
    pends = jnp.cumsum(padded)
    dest = (pends - padded)[flat_e] + rank
    nb_max = pl.cdiv(n_assign, bm) + N_EXPERTS
    flat_t = jnp.tile(jnp.arange(s, dtype=jnp.int32), TOP_K)
    slot_tok = (jnp.arange(nb_max * bm, dtype=jnp.int32) % s).at[dest].set(flat_t)
    nb_used = (pends[-1] // bm).astype(jnp.int32).reshape(1)
    block_start = jnp.arange(nb_max, dtype=jnp.int32) * bm
    block_e = jnp.minimum(jnp.sum((pends[None, :] <= block_start[:, None]).astype(jnp.int32), axis=1),
                          N_EXPERTS - 1)
    return dest.astype(jnp.int32), slot_tok, block_e, nb_used, nb_max


def _expert_schedule(block_e, nb_used):
    nb_max = block_e.shape[0]
    idx = jnp.arange(nb_max, dtype=jnp.int32)
    used = idx < nb_used[0]
    prev_e = jnp.concatenate([jnp.full((1,), -1, jnp.int32), block_e[:-1]])
    first = (used & (block_e != prev_e)).astype(jnp.int32)
    pos = jnp.where(first == 1, idx, nb_max)
    after = jnp.concatenate([lax.cummin(pos[::-1])[::-1][1:], jnp.full((1,), nb_max, jnp.int32)])
    nxt = jnp.where(after < nb_max, block_e[jnp.minimum(after, nb_max - 1)], -1)
    return first, nxt.astype(jnp.int32)


def _expert_body(be_ref, first_ref, nxt_ref, st_ref, nu_ref, h_hbm, wg_hbm, wu_hbm, wd_hbm, y_ref,
                 xa_ref, xb_ref, wg_st, wu_st, wd_st, wg_bf, wu_bf, wd_bf, gsem, wsem, *, bm, layer):
    b = pl.program_id(0)
    nb = nu_ref[0]
    n_chunks = D_FF_EXPERT // MOE_FF_CHUNK

    def row_copy(tok, r, buf, si):
        return pltpu.make_async_copy(h_hbm.at[pl.ds(tok, 1), :], buf.at[pl.ds(r, 1), :], gsem.at[si])

    def wait_gather(buf, si):
        pltpu.make_async_copy(h_hbm.at[pl.ds(0, bm), :], buf, gsem.at[si]).wait()

    def weight_copies(e):
        return (pltpu.make_async_copy(wg_hbm.at[layer, e], wg_st, wsem.at[0]),
                pltpu.make_async_copy(wu_hbm.at[layer, e], wu_st, wsem.at[1]),
                pltpu.make_async_copy(wd_hbm.at[layer, e], wd_st, wsem.at[2]))

    @pl.when(b == 0)
    def _():
        for cp in weight_copies(be_ref[0]):
            cp.start()

        def go(r, c):
            row_copy(st_ref[r], r, xa_ref, 0).start()
            return c
        lax.fori_loop(0, bm, go, 0, unroll=MOE_GATHER_UNROLL)

    @pl.when((b < nb) & (first_ref[b] == 1))
    def _():
        stagings = ((wg_st, wg_bf, 1), (wu_st, wu_bf, 1), (wd_st, wd_bf, 0))
        for m, (st, bf, axis) in enumerate(stagings):
            weight_copies(be_ref[b])[m].wait()
            for c in range(n_chunks):
                cs = slice(c * MOE_FF_CHUNK, (c + 1) * MOE_FF_CHUNK)
                if axis == 1:
                    bf[:, cs] = st[:, cs].astype(BF16)
                else:
                    bf[cs, :] = st[cs, :].astype(BF16)

            @pl.when(nxt_ref[b] >= 0)
            def _():
                weight_copies(nxt_ref[b])[m].start()

    def step(cur, cur_si, nxt_buf, nxt_si):
        wait_gather(cur, cur_si)
        base = jnp.minimum(b + 1, nb - 1) * bm
        x = cur[...].astype(BF16)
        y = None
        for c in range(n_chunks):
            for r in range(c * bm // n_chunks, (c + 1) * bm // n_chunks):
                row_copy(st_ref[base + r], r, nxt_buf, nxt_si).start(priority=r % 2)
            cs = slice(c * MOE_FF_CHUNK, (c + 1) * MOE_FF_CHUNK)
            gate = jnp.dot(x, wg_bf[:, cs], preferred_element_type=F32)
            up = jnp.dot(x, wu_bf[:, cs], preferred_element_type=F32)
            mid = (gate * jax.nn.sigmoid(gate) * up).astype(BF16)
            part = jnp.dot(mid, wd_bf[cs, :], preferred_element_type=F32)
            y = part if y is None else y + part
        y_ref[...] = y

        @pl.when(b == nb - 1)
        def _():
            wait_gather(nxt_buf, nxt_si)

    @pl.when((b < nb) & (b % 2 == 0))
    def _():
        step(xa_ref, 0, xb_ref, 1)

    @pl.when((b < nb) & (b % 2 == 1))
    def _():
        step(xb_ref, 1, xa_ref, 0)

    @pl.when(b >= nb)
    def _():
        y_ref[...] = jnp.zeros_like(y_ref)


def _experts(h, slot_tok, block_e, nb_used, nb_max, layer, w_gate, w_up, w_down, bm):
    d = h.shape[1]
    ff = D_FF_EXPERT
    first, nxt = _expert_schedule(block_e, nb_used)
    hbm = pl.BlockSpec(memory_space=pl.ANY)
    grid_spec = pltpu.PrefetchScalarGridSpec(
        num_scalar_prefetch=5,
        grid=(nb_max,),
        in_specs=[hbm, hbm, hbm, hbm],
        out_specs=pl.BlockSpec((bm, d), lambda b, *_: (b, 0)),
        scratch_shapes=[pltpu.VMEM((bm, d), F32), pltpu.VMEM((bm, d), F32),
                        pltpu.VMEM((d, ff), F32), pltpu.VMEM((d, ff), F32), pltpu.VMEM((ff, d), F32),
                        pltpu.VMEM((d, ff), BF16), pltpu.VMEM((d, ff), BF16), pltpu.VMEM((ff, d), BF16),
                        pltpu.SemaphoreType.DMA((2,)), pltpu.SemaphoreType.DMA((3,))],
    )
    return pl.pallas_call(
        functools.partial(_expert_body, bm=bm, layer=layer),
        out_shape=jax.ShapeDtypeStruct((nb_max * bm, d), F32),
        grid_spec=grid_spec,
        compiler_params=pltpu.CompilerParams(dimension_semantics=("arbitrary",),
                                             vmem_limit_bytes=VMEM_LIMIT),
        name="experts",
    )(block_e, first, nxt, slot_tok, nb_used, h, w_gate, w_up, w_down)


def _combine_body(pos_ref, y_hbm, w_ref, x_ref, g2_ref, ng_ref, *rest, tc, s, final):
    if final:
        o_ref, yb_ref, sem = rest
    else:
        nsc_ref, nsh_ref, gw_ref, gb_ref, o_ref, h_ref, gate_ref, yb_ref, sem = rest
    i = pl.program_id(0)
    n = pl.num_programs(0)

    def start_gather(blk, slot):
        for k in range(TOP_K):
            for r in range(tc):
                src = pos_ref[k * s + blk * tc + r]
                pltpu.make_async_copy(y_hbm.at[pl.ds(src, 1), :], yb_ref.at[slot, k, pl.ds(r, 1), :],
                                      sem.at[slot]).start(priority=k)

    def wait_gather(slot):
        for k in range(TOP_K):
            pltpu.make_async_copy(y_hbm.at[pl.ds(0, tc), :], yb_ref.at[slot, k], sem.at[slot]).wait()

    slot = i % 2

    @pl.when(i == 0)
    def _():
        start_gather(0, 0)

    @pl.when(i + 1 < n)
    def _():
        start_gather(i + 1, 1 - slot)

    wait_gather(slot)
    w = w_ref[...]
    moe = w[:, 0:1] * yb_ref[slot, 0] + w[:, 1:2] * yb_ref[slot, 1]
    xn = x_ref[...] + g2_ref[...] * moe
    if final:
        o_ref[...] = (xn * lax.rsqrt(jnp.mean(xn * xn, axis=-1, keepdims=True) + EPS)) * ng_ref[...]
    else:
        o_ref[...] = xn
        h = _norm_mod(xn, ng_ref[...], nsc_ref[...], nsh_ref[...]).astype(h_ref.dtype)
        h_ref[...] = h
        gate_ref[...] = _gate_rows(h, gw_ref[0], gb_ref[...])


def _combine(ys, pos, wts, x, g2, tc, norm_g, next_layer=None):
    s, d = x.shape
    final = next_layer is None
    row = pl.BlockSpec((tc, d), lambda i, pos: (i, 0))
    vec = pl.BlockSpec((1, d), lambda i, pos: (0, 0))
    if final:
        extra_args, extra_specs = (), []
    else:
        norm_sc, norm_sh, w_in_t, gate_bias, layer = next_layer
        extra_args = (norm_sc, norm_sh, w_in_t, gate_bias)
        extra_specs = [vec, vec, _gate_w_spec(layer), pl.BlockSpec((GATE_COLS, 1), lambda i, pos: (0, 0))]
    grid_spec = pltpu.PrefetchScalarGridSpec(
        num_scalar_prefetch=1,
        grid=(s // tc,),
        in_specs=[pl.BlockSpec(memory_space=pl.ANY), pl.BlockSpec((tc, TOP_K), lambda i, pos: (i, 0)), row, vec, vec]
        + extra_specs,
        out_specs=row if final else (row, row, pl.BlockSpec((GATE_COLS, tc), lambda i, pos: (0, i))),
        scratch_shapes=[pltpu.VMEM((2, TOP_K, tc, d), F32), pltpu.SemaphoreType.DMA((2,))],
    )
    out_f32 = jax.ShapeDtypeStruct((s, d), F32)
    return pl.pallas_call(
        functools.partial(_combine_body, tc=tc, s=s, final=final),
        out_shape=out_f32 if final else (out_f32, jax.ShapeDtypeStruct((s, d), BF16),
                                         jax.ShapeDtypeStruct((GATE_COLS, s), F32)),
        grid_spec=grid_spec,
        compiler_params=_cparams(("arbitrary",)),
        name="moe_combine",
    )(pos, ys, wts, x, g2, norm_g, *extra_args)


def kernel(x, c, ada_w, ada_b, norm1_g, w_in, m_conv_w, m_conv_b, m_igate_b, m_fgate_b, m_norm_g, w_out,
           norm2_g, router_w, router_bias, w_gate, w_up, w_down, final_g):
    batch, s, d = x.shape
    assert batch == 1 and d == D_MODEL
    assert MLSTM_CHUNK == COMBINE_TC
    chunk = min(MLSTM_CHUNK, s)
    sb_tile = min(SB_TILE, s)
    tc = chunk
    xs = x.reshape(s, d)
    mod = _ada(c, ada_w, ada_b)
    router_wt = router_w.T
    w_in_t = jnp.swapaxes(w_in, 1, 2)
    mods = [[mod[l, :, i * d:(i + 1) * d] for i in range(6)] for l in range(DEPTH)]
    gate_bias = [jnp.concatenate([m_igate_b[l], m_fgate_b[l]]).reshape(GATE_COLS, 1) for l in range(DEPTH)]
    h1, gproc = _norm1(xs, norm1_g[0].reshape(1, d), mods[0][1], mods[0][0], w_in_t, gate_bias[0], 0, chunk)
    for l in range(DEPTH):
        sh1, sc1, g1, sh2, sc2, g2 = mods[l]
        p = _in_proj(h1, w_in_t, l)
        g_rows = gproc.reshape(2, M_HEADS, s).transpose(1, 0, 2)
        g_cols = g_rows.transpose(0, 2, 1)
        hm = _mlstm(p, m_conv_w[l], m_conv_b[l].reshape(1, -1), g_rows, g_cols, chunk)
        hs = _stick_breaking(p, sb_tile)
        xs = _out_proj(p, hm, hs, m_norm_g[l].reshape(1, d), xs, g1, w_out[l].astype(BF16))
        h2, eid, wts = _router(xs, norm2_g[l].reshape(1, d), sc2, sh2, router_wt, router_bias)
        dest, slot_tok, block_e, nb_used, nb_max = _dispatch(eid, MOE_BM)
        ys = _experts(h2, slot_tok, block_e, nb_used, nb_max, l, w_gate, w_up, w_down, MOE_BM)
        if l + 1 < DEPTH:
            xs, h1, gproc = _combine(ys, dest, wts.T, xs, g2, tc, norm1_g[l + 1].reshape(1, d),
                                     (mods[l + 1][1], mods[l + 1][0], w_in_t, gate_bias[l + 1], l + 1))
        else:
            out = _combine(ys, dest, wts.T, xs, g2, tc, final_g.reshape(1, d))
    return out.reshape(batch, s, d)
```
